```python
import math
import jax, jax.numpy as jnp
from jax import lax
import numpy as np

D_MODEL = 1024
BATCH = 8
SEQ = 2048
DEPTH = 4

MEM_LEN = 256
D_FF = 2816
CONV_D = D_MODEL
CONV_K = 31
N_HEADS = 8
HEAD_DIM = 64
V_DIM = 2 * HEAD_DIM
QK_D = N_HEADS * 2 * HEAD_DIM
ATTN_D = N_HEADS * V_DIM
Q_BLOCK = 128
N_BUCKETS = 32
MAX_DISTANCE = 128
POOL_WINDOWS = (2, 4, 8, 16)
POOL_GROUPS = len(POOL_WINDOWS)
POOL_D = D_MODEL
POOL_GD = POOL_D // POOL_GROUPS
X_HEADS = 4
X_HEAD_DIM = D_MODEL // X_HEADS
X_D = X_HEADS * X_HEAD_DIM
N_BRANCH = 3
SPLITS = (2 * CONV_D,
          2 * CONV_D + QK_D,
          2 * CONV_D + 2 * QK_D,
          2 * CONV_D + 2 * QK_D + ATTN_D,
          2 * CONV_D + 2 * QK_D + ATTN_D + POOL_D)
IN_COLS = SPLITS[-1] + N_BRANCH * D_MODEL
RMS_EPS = 1e-6
LN_EPS = 1e-5
NEG_INF = -1e30

kernel_name = "hybrid_conv_diffattn_pool_macaron_trunk"


def rms_norm(x, g):
    xf = x.astype(jnp.float32)
    y = xf * lax.rsqrt(jnp.mean(xf * xf, axis=-1, keepdims=True) + RMS_EPS)
    return (y * g.astype(jnp.float32)).astype(x.dtype)


def layer_norm(x, g, b):
    xf = x.astype(jnp.float32)
    mu = jnp.mean(xf, axis=-1, keepdims=True)
    xc = xf - mu
    y = xc * lax.rsqrt(jnp.mean(xc * xc, axis=-1, keepdims=True) + LN_EPS)
    return (y * g.astype(jnp.float32) + b.astype(jnp.float32)).astype(x.dtype)


def swiglu(h, w1, w3, w2):
    return (jax.nn.silu(h @ w1) * (h @ w3)) @ w2


def t5_bucket(n):
    n = jnp.maximum(n, 0)
    max_exact = N_BUCKETS // 2
    nf = jnp.maximum(n, 1).astype(jnp.float32)
    large = max_exact + (jnp.log(nf / max_exact) / math.log(MAX_DISTANCE / max_exact)
                         * (N_BUCKETS - max_exact)).astype(jnp.int32)
    large = jnp.minimum(large, N_BUCKETS - 1)
    return jnp.where(n < max_exact, n, large)


def conv_branch(u, w_dw, b_dw, ln_g, ln_b, w_pw):
    a, gt = jnp.split(u, 2, axis=-1)
    z = a * jax.nn.sigmoid(gt)
    z = lax.conv_general_dilated(z, w_dw[:, None, :], window_strides=(1,),
                                 padding=((CONV_K - 1, 0),),
                                 dimension_numbers=('NWC', 'WIO', 'NWC'),
                                 feature_group_count=CONV_D) + b_dw
    z = jax.nn.silu(layer_norm(z, ln_g, ln_b))
    return z @ w_pw


def diff_attention(q, k, v, lam, lam_init, subln_g, rel_bias):
    B, S = q.shape[0], q.shape[1]
    lamf = lam.astype(jnp.float32)
    lam_full = (jnp.exp(jnp.sum(lamf[0] * lamf[1])) - jnp.exp(jnp.sum(lamf[2] * lamf[3])) + lam_init)
    scale = HEAD_DIM ** -0.5
    n_blocks = S // Q_BLOCK
    k_pos = jnp.arange(S)
    qb = q.reshape(B, n_blocks, Q_BLOCK, N_HEADS, 2, HEAD_DIM).transpose(1, 0, 2, 3, 4, 5)

    def block(args):
        i, qi = args
        q_pos = i * Q_BLOCK + jnp.arange(Q_BLOCK)
        rel = q_pos[:, None] - k_pos[None, :]
        bias = rel_bias.astype(jnp.float32)[t5_bucket(rel)]
        logits = jnp.einsum('bqhmd,bkhmd->bhmqk', qi, k).astype(jnp.float32) * scale
        logits = logits + bias.transpose(2, 0, 1)[None, :, None]
        logits = jnp.where((rel >= 0)[None, None, None], logits, NEG_INF)
        p = jax.nn.softmax(logits, axis=-1)
        p = p[:, :, 0] - lam_full * p[:, :, 1]
        return jnp.einsum('bhqk,bkhv->bqhv', p.astype(v.dtype), v)

    out = lax.map(block, (jnp.arange(n_blocks), qb))
    out = out.transpose(1, 0, 2, 3, 4).reshape(B, S, N_HEADS, V_DIM)
    out = rms_norm(out, subln_g) * (1.0 - lam_init)
    return out.reshape(B, S, ATTN_D)


def pool_branch(u, w_grp, scale, w_o):
    B, S = u.shape[0], u.shape[1]
    uf = u.astype(jnp.float32).reshape(B, S, POOL_GROUPS, POOL_GD)
    cz = jnp.pad(jnp.cumsum(uf, axis=1), ((0, 0), (1, 0), (0, 0), (0, 0)))
    t = jnp.arange(S)
    outs = []
    for g, w in enumerate(POOL_WINDOWS):
        c = cz[:, :, g]
        prev = jnp.pad(c, ((0, 0), (w - 1, 0), (0, 0)))[:, :S]
        cnt = jnp.minimum(t + 1, w).astype(jnp.float32)[None, :, None]
        outs.append((c[:, 1:] - prev) / cnt)
    pooled = (jnp.stack(outs, axis=2) - uf).astype(u.dtype)
    y = jnp.einsum('bsgc,gcd->bsgd', pooled, w_grp).reshape(B, S, POOL_D) * scale
    return y @ w_o


def cross_attention(h, mem_n, wq, wkv, wo):
    B, S = h.shape[0], h.shape[1]
    M = mem_n.shape[1]
    q = (h @ wq).reshape(B, S, X_HEADS, X_HEAD_DIM)
    k, v = jnp.split(mem_n @ wkv, 2, axis=-1)
    k = k.reshape(B, M, X_HEADS, X_HEAD_DIM)
    v = v.reshape(B, M, X_HEADS, X_HEAD_DIM)
    logits = jnp.einsum('bshd,bmhd->bhsm', q, k).astype(jnp.float32) * (X_HEAD_DIM ** -0.5)
    p = jax.nn.softmax(logits, axis=-1)
    o = jnp.einsum('bhsm,bmhd->bshd', p.astype(v.dtype), v).reshape(B, S, X_D)
    return o @ wo


def mixer_block(h, w_in, conv_dw, conv_dw_b, conv_ln_g, conv_ln_b, conv_pw,
                lam, lam_init, subln_g, attn_o, rel_bias, pool_w, pool_scale, pool_o, w_out):
    B, S = h.shape[0], h.shape[1]
    u = h @ w_in
    u_conv, q, k, v, u_pool, gates = jnp.split(u, SPLITS, axis=-1)
    y_conv = conv_branch(u_conv, conv_dw, conv_dw_b, conv_ln_g, conv_ln_b, conv_pw)
    q = q.reshape(B, S, N_HEADS, 2, HEAD_DIM)
    k = k.reshape(B, S, N_HEADS, 2, HEAD_DIM)
    v = v.reshape(B, S, N_HEADS, V_DIM)
    y_attn = diff_attention(q, k, v, lam, lam_init, subln_g, rel_bias) @ attn_o
    y_pool = pool_branch(u_pool, pool_w, pool_scale, pool_o)
    g = jax.nn.sigmoid(gates.reshape(B, S, N_BRANCH, D_MODEL))
    merged = g[:, :, 0] * y_conv + g[:, :, 1] * y_attn + g[:, :, 2] * y_pool
    return merged @ w_out


def setup_inputs(seed: int = 0) -> dict:
    key = jax.random.key(seed)
    ks = jax.random.split(key, 24)
    f32 = jnp.float32

    def nrm(k, shape, scale):
        return jax.random.normal(k, shape, f32) * scale

    L = DEPTH
    return {
        "x": nrm(ks[0], (BATCH, SEQ, D_MODEL), 1.0),
        "mem": nrm(ks[1], (BATCH, MEM_LEN, D_MODEL), 1.0),
        "norm_g": 1.0 + nrm(ks[2], (L, 8, D_MODEL), 0.05),
        "ffn_w1": nrm(ks[3], (L, 2, D_MODEL, D_FF), D_MODEL ** -0.5),
        "ffn_w3": nrm(ks[4], (L, 2, D_MODEL, D_FF), D_MODEL ** -0.5),
        "ffn_w2": nrm(ks[5], (L, 2, D_FF, D_MODEL), D_FF ** -0.5),
        "w_in": nrm(ks[6], (L, D_MODEL, IN_COLS), D_MODEL ** -0.5),
        "conv_dw": nrm(ks[7], (L, CONV_K, CONV_D), CONV_K ** -0.5),
        "conv_dw_b": nrm(ks[8], (L, CONV_D), 0.02),
        "conv_ln_g": 1.0 + nrm(ks[9], (L, CONV_D), 0.05),
        "conv_ln_b": nrm(ks[10], (L, CONV_D), 0.02),
        "conv_pw": nrm(ks[11], (L, CONV_D, D_MODEL), CONV_D ** -0.5),
        "attn_lam": nrm(ks[12], (L, 4, HEAD_DIM), 0.1),
        "attn_subln": 1.0 + nrm(ks[13], (L, V_DIM), 0.05),
        "attn_o": nrm(ks[14], (L, ATTN_D, D_MODEL), ATTN_D ** -0.5),
        "rel_bias": nrm(ks[15], (N_BUCKETS, N_HEADS), 0.5),
        "pool_w": nrm(ks[16], (L, POOL_GROUPS, POOL_GD, POOL_GD), POOL_GD ** -0.5),
        "pool_scale": 1.0 + nrm(ks[17], (L, POOL_D), 0.1),
        "pool_o": nrm(ks[18], (L, POOL_D, D_MODEL), POOL_D ** -0.5),
        "w_out": nrm(ks[19], (L, D_MODEL, D_MODEL), D_MODEL ** -0.5),
        "mem_norm": 1.0 + nrm(ks[20], (L, D_MODEL), 0.05),
        "xattn_q": nrm(ks[21], (L, D_MODEL, X_D), D_MODEL ** -0.5),
        "xattn_kv": nrm(ks[22], (L, D_MODEL, 2 * X_D), D_MODEL ** -0.5),
        "xattn_o": nrm(ks[23], (L, X_D, D_MODEL), X_D ** -0.5),
    }


def reference(x, mem, norm_g, ffn_w1, ffn_w3, ffn_w2, w_in, conv_dw, conv_dw_b, conv_ln_g, conv_ln_b,
              conv_pw, attn_lam, attn_subln, attn_o, rel_bias, pool_w, pool_scale, pool_o, w_out,
              mem_norm, xattn_q, xattn_kv, xattn_o):
    for l in range(DEPTH):
        ng = norm_g[l]
        lam_init = 0.8 - 0.6 * math.exp(-0.3 * l)
        h = rms_norm(x, ng[0])
        x = x + 0.5 * rms_norm(swiglu(h, ffn_w1[l, 0], ffn_w3[l, 0], ffn_w2[l, 0]), ng[1])
        h = rms_norm(x, ng[2])
        y = mixer_block(h, w_in[l], conv_dw[l], conv_dw_b[l], conv_ln_g[l], conv_ln_b[l], conv_pw[l],
                        attn_lam[l], lam_init, attn_subln[l], attn_o[l], rel_bias,
                        pool_w[l], pool_scale[l], pool_o[l], w_out[l])
        x = x + rms_norm(y, ng[3])
        h = rms_norm(x, ng[4])
        y = cross_attention(h, rms_norm(mem, mem_norm[l]), xattn_q[l], xattn_kv[l], xattn_o[l])
        x = x + rms_norm(y, ng[5])
        h = rms_norm(x, ng[6])
        x = x + 0.5 * rms_norm(swiglu(h, ffn_w1[l, 1], ffn_w3[l, 1], ffn_w2[l, 1]), ng[7])
    return x
```

```python
import functools
import math

import jax
import jax.numpy as jnp
from jax import lax
from jax.experimental import pallas as pl
from jax.experimental.pallas import tpu as pltpu

D_MODEL = 1024
DEPTH = 4
D_FF = 2816
CONV_K = 31
N_HEADS = 8
HEAD_DIM = 64
V_DIM = 2 * HEAD_DIM
N_BUCKETS = 32
MAX_DISTANCE = 128
POOL_WINDOWS = (2, 4, 8, 16)
POOL_GD = D_MODEL // len(POOL_WINDOWS)
X_HEADS = 4
X_HEAD_DIM = D_MODEL // X_HEADS
COL_CONV_A = 0
COL_CONV_G = D_MODEL
COL_QKV = 2 * D_MODEL
COL_POOL = 5 * D_MODEL
COL_GATES = 6 * D_MODEL
RMS_EPS = 1e-6
LN_EPS = 1e-5
NEG_INF = -1e30

F32 = jnp.float32
BF16 = jnp.bfloat16

HALO = 32
FFN_TM = 512
FFN_TF = 256
PROJ_TM = 512
SEQ_TS = 256
ATT_T = 256
VMEM_LIMIT = 56 * 1024 * 1024


def _rms(x, g):
    ms = jnp.mean(x * x, axis=-1, keepdims=True)
    return x * lax.rsqrt(ms + RMS_EPS) * g


def _dot(a, b):
    return jnp.dot(a, b, preferred_element_type=F32)


def _params(sem):
    return pltpu.CompilerParams(dimension_semantics=sem, vmem_limit_bytes=VMEM_LIMIT)


def _resident(shape, index_map):
    return pl.BlockSpec(shape, index_map, pipeline_mode=pl.Buffered(1))


def _ffn_kernel(x_ref, ng_ref, w1_ref, w3_ref, w2_ref, o_ref, *, g_in, g_out):
    x = x_ref[...]
    h = _rms(x, ng_ref[g_in:g_in + 1, :]).astype(BF16)
    acc = None
    for c in range(D_FF // FFN_TF):
        sl = slice(c * FFN_TF, (c + 1) * FFN_TF)
        a = _dot(h, w1_ref[:, sl])
        b = _dot(h, w3_ref[:, sl])
        t = (a * jax.nn.sigmoid(a) * b).astype(BF16)
        y = _dot(t, w2_ref[sl, :])
        acc = y if acc is None else acc + y
    o_ref[...] = x + 0.5 * _rms(acc, ng_ref[g_out:g_out + 1, :])


def _ffn(x2, ng, w1, w3, w2, l, s):
    n = x2.shape[0]
    g_in, g_out = (0, 1) if s == 0 else (6, 7)
    return pl.pallas_call(
        functools.partial(_ffn_kernel, g_in=g_in, g_out=g_out),
        grid=(n // FFN_TM,),
        in_specs=[
            pl.BlockSpec((FFN_TM, D_MODEL), lambda i: (i, 0)),
            _resident((None, 8, D_MODEL), lambda i: (l, 0, 0)),
            _resident((None, None, D_MODEL, D_FF), lambda i: (l, s, 0, 0)),
            _resident((None, None, D_MODEL, D_FF), lambda i: (l, s, 0, 0)),
            _resident((None, None, D_FF, D_MODEL), lambda i: (l, s, 0, 0)),
        ],
        out_specs=pl.BlockSpec((FFN_TM, D_MODEL), lambda i: (i, 0)),
        out_shape=jax.ShapeDtypeStruct((n, D_MODEL), F32),
        compiler_params=_params(("parallel",)),
        name="ffn",
    )(x2, ng, w1, w3, w2)


def _qkv_kernel(x_ref, ng_ref, w_ref, o_ref, h_ref):
    j = pl.program_id(1)

    @pl.when(j == 0)
    def _():
        h_ref[...] = _rms(x_ref[...], ng_ref[2:3, :]).astype(BF16)

    scale = jnp.where(j == 0, HEAD_DIM ** -0.5, 1.0).astype(F32)
    o_ref[...] = (_dot(h_ref[...], w_ref[...]) * scale).astype(o_ref.dtype)


def _qkv_proj(x2, ng, w_in, l):
    n = x2.shape[0]
    tn = D_MODEL
    return pl.pallas_call(
        _qkv_kernel,
        grid=(n // PROJ_TM, 3),
        in_specs=[
            pl.BlockSpec((PROJ_TM, D_MODEL), lambda i, j: (i, 0)),
            _resident((None, 8, D_MODEL), lambda i, j: (l, 0, 0)),
            pl.BlockSpec((None, D_MODEL, tn), lambda i, j: (l, 0, COL_QKV // tn + j)),
        ],
        out_specs=pl.BlockSpec((PROJ_TM, tn), lambda i, j: (i, j)),
        out_shape=jax.ShapeDtypeStruct((n, 3 * D_MODEL), BF16),
        scratch_shapes=[pltpu.VMEM((PROJ_TM, D_MODEL), BF16)],
        compiler_params=_params(("parallel", "arbitrary")),
        name="qkv_proj",
    )(x2, ng, w_in)


def _convpool_proj_kernel(x_ref, ng_ref, wa_ref, wg_ref, wp_ref, z_ref, u_ref):
    h = _rms(x_ref[...], ng_ref[2:3, :]).astype(BF16)
    a = _dot(h, wa_ref[...])
    gt = _dot(h, wg_ref[...])
    z_ref[...] = a * jax.nn.sigmoid(gt)
    u_ref[...] = _dot(h, wp_ref[...])


def _convpool_proj(x2, ng, w_in, l):
    n = x2.shape[0]
    tn = D_MODEL
    row = pl.BlockSpec((PROJ_TM, D_MODEL), lambda i: (i, 0))
    return pl.pallas_call(
        _convpool_proj_kernel,
        grid=(n // PROJ_TM,),
        in_specs=[
            row,
            _resident((None, 8, D_MODEL), lambda i: (l, 0, 0)),
            _resident((None, D_MODEL, tn), lambda i: (l, 0, COL_CONV_A // tn)),
            _resident((None, D_MODEL, tn), lambda i: (l, 0, COL_CONV_G // tn)),
            _resident((None, D_MODEL, tn), lambda i: (l, 0, COL_POOL // tn)),
        ],
        out_specs=[row, row],
        out_shape=[jax.ShapeDtypeStruct((n, D_MODEL), F32)] * 2,
        compiler_params=_params(("parallel",)),
        name="convpool_proj",
    )(x2, ng, w_in, w_in, w_in)


def _bias_kernel(rb_ref, o_ref):
    h = pl.program_id(0)
    d = pl.program_id(1)
    t = ATT_T
    i = lax.broadcasted_iota(jnp.int32, (t, t), 0)
    j = lax.broadcasted_iota(jnp.int32, (t, t), 1)
    rel = d * t + i - j
    n = jnp.maximum(rel, 0)
    max_exact = N_BUCKETS // 2
    nf = jnp.maximum(n, 1).astype(F32)
    large = max_exact + (jnp.log(nf / max_exact) / math.log(MAX_DISTANCE / max_exact)
                         * (N_BUCKETS - max_exact)).astype(jnp.int32)
    large = jnp.minimum(large, N_BUCKETS - 1)
    bucket = jnp.where(n < max_exact, n, large)
    val = jnp.zeros((t, t), F32)
    for b in range(N_BUCKETS):
        val = jnp.where(bucket == b, rb_ref[b, h], val)
    val = val - rb_ref[N_BUCKETS - 1, h]
    val = jnp.where(rel >= 0, val, NEG_INF)
    o_ref[0:t, :] = val
    o_ref[t:2 * t, :] = val


def _bias_tiles(rel_bias):
    t = ATT_T
    return pl.pallas_call(
        _bias_kernel,
        grid=(N_HEADS, 2),
        in_specs=[pl.BlockSpec(memory_space=pltpu.SMEM)],
        out_specs=pl.BlockSpec((None, None, 2 * t, t), lambda h, d: (h, d, 0, 0)),
        out_shape=jax.ShapeDtypeStruct((N_HEADS, 2, 2 * t, t), F32),
        compiler_params=_params(("arbitrary", "arbitrary")),
        name="bias_tiles",
    )(rel_bias)


def _attn_kernel(q_ref, k_ref, v_ref, bias_ref, lam_ref, sg_ref, o_ref,
                 q2_ref, m_ref, l_ref, acc_ref, *, lam_init):
    t = ATT_T
    qi = pl.program_id(2)
    q = q_ref[...]
    lane = lax.broadcasted_iota(jnp.int32, (t, V_DIM), 1)
    zero = jnp.zeros_like(q)
    q2_ref[0:t, :] = jnp.where(lane < HEAD_DIM, q, zero)
    q2_ref[t:2 * t, :] = jnp.where(lane >= HEAD_DIM, q, zero)
    m_ref[...] = jnp.full(m_ref.shape, -jnp.inf, F32)
    l_ref[...] = jnp.zeros(l_ref.shape, F32)
    acc_ref[...] = jnp.zeros(acc_ref.shape, F32)

    def step(ki, bias):
        start = pl.multiple_of(ki * t, t)
        k = k_ref[pl.ds(start, t), :]
        v = v_ref[pl.ds(start, t), :]
        s = lax.dot_general(q2_ref[...], k, (((1,), (1,)), ((), ())),
                            preferred_element_type=F32)
        if bias is not None:
            s = s + bias
        m_prev = m_ref[...]
        m_next = jnp.maximum(m_prev, jnp.max(s, axis=1, keepdims=True))
        alpha = jnp.exp(m_prev - m_next)
        p = jnp.exp(s - pltpu.repeat(m_next, t // 128, axis=1))
        l_ref[...] = alpha * l_ref[...] + jnp.sum(p, axis=1, keepdims=True)
        acc_ref[...] = alpha * acc_ref[...] + _dot(p.astype(BF16), v)
        m_ref[...] = m_next

    def far(ki, carry):
        step(ki, None)
        return carry

    lax.fori_loop(0, jnp.maximum(qi - 1, 0), far, 0)

    @pl.when(qi >= 1)
    def _():
        step(qi - 1, bias_ref[1])

    step(qi, bias_ref[0])

    o = acc_ref[...] / l_ref[...]
    lam = lam_ref[...]
    lam_full = (jnp.exp(jnp.sum(lam[0:1, :] * lam[1:2, :], axis=-1, keepdims=True))
                - jnp.exp(jnp.sum(lam[2:3, :] * lam[3:4, :], axis=-1, keepdims=True)) + lam_init)
    out = o[0:t, :] - lam_full * o[t:2 * t, :]
    o_ref[...] = (_rms(out, sg_ref[...]) * (1.0 - lam_init)).astype(o_ref.dtype)


def _diff_attention(qkv, bias, lam, subln, lam_init):
    b, s, _ = qkv.shape
    t = ATT_T
    return pl.pallas_call(
        functools.partial(_attn_kernel, lam_init=lam_init),
        grid=(b, N_HEADS, s // t),
        in_specs=[
            pl.BlockSpec((None, t, V_DIM), lambda bi, h, qi: (bi, qi, h)),
            pl.BlockSpec((None, s, V_DIM), lambda bi, h, qi: (bi, 0, N_HEADS + h)),
            pl.BlockSpec((None, s, V_DIM), lambda bi, h, qi: (bi, 0, 2 * N_HEADS + h)),
            pl.BlockSpec((None, 2, 2 * t, t), lambda bi, h, qi: (h, 0, 0, 0)),
            pl.BlockSpec((4, HEAD_DIM), lambda bi, h, qi: (0, 0)),
            pl.BlockSpec((1, V_DIM), lambda bi, h, qi: (0, 0)),
        ],
        out_specs=pl.BlockSpec((None, t, V_DIM), lambda bi, h, qi: (bi, qi, h)),
        out_shape=jax.ShapeDtypeStruct((b, s, N_HEADS * V_DIM), BF16),
        scratch_shapes=[
            pltpu.VMEM((2 * t, V_DIM), BF16),
            pltpu.VMEM((2 * t, 128), F32),
            pltpu.VMEM((2 * t, 128), F32),
            pltpu.VMEM((2 * t, V_DIM), F32),
        ],
        compiler_params=_params(("parallel", "parallel", "arbitrary")),
        name="diff_attention",
    )(qkv, qkv, qkv, bias, lam, subln)


def _conv_kernel(z_ref, zh_ref, w_ref, b_ref, lg_ref, lb_ref, o_ref, ext_ref, cv_ref):
    ts = SEQ_TS
    i = pl.program_id(1)
    ext_ref[0:HALO, :] = jnp.where(i == 0, 0.0, zh_ref[...])
    ext_ref[HALO:HALO + ts, :] = z_ref[...]
    rc = 64
    off = HALO - (CONV_K - 1)

    def lanes(lc, carry):
        l0 = pl.multiple_of(lc * 128, 128)
        for r in range(ts // rc):
            acc = None
            for k in range(CONV_K):
                term = w_ref[k:k + 1, pl.ds(l0, 128)] * ext_ref[r * rc + off + k:r * rc + off + k + rc, pl.ds(l0, 128)]
                acc = term if acc is None else acc + term
            cv_ref[r * rc:(r + 1) * rc, pl.ds(l0, 128)] = acc
        return carry

    lax.fori_loop(0, D_MODEL // 128, lanes, 0)
    c = cv_ref[...] + b_ref[...]
    mu = jnp.mean(c, axis=-1, keepdims=True)
    xc = c - mu
    y = xc * lax.rsqrt(jnp.mean(xc * xc, axis=-1, keepdims=True) + LN_EPS) * lg_ref[...] + lb_ref[...]
    o_ref[...] = (y * jax.nn.sigmoid(y)).astype(o_ref.dtype)


def _halo_spec(ts):
    return pl.BlockSpec((None, HALO, D_MODEL),
                        lambda b, i: (b, jnp.maximum(i * (ts // HALO) - 1, 0), 0))


def _conv_branch(z, w_dw, b_dw, ln_g, ln_b):
    b, s, _ = z.shape
    ts = SEQ_TS
    vec = pl.BlockSpec((1, D_MODEL), lambda bi, i: (0, 0))
    return pl.pallas_call(
        _conv_kernel,
        grid=(b, s // ts),
        in_specs=[
            pl.BlockSpec((None, ts, D_MODEL), lambda bi, i: (bi, i, 0)),
            _halo_spec(ts),
            pl.BlockSpec((CONV_K, D_MODEL), lambda bi, i: (0, 0)),
            vec, vec, vec,
        ],
        out_specs=pl.BlockSpec((None, ts, D_MODEL), lambda bi, i: (bi, i, 0)),
        out_shape=jax.ShapeDtypeStruct((b, s, D_MODEL), BF16),
        scratch_shapes=[
            pltpu.VMEM((HALO + ts, D_MODEL), F32),
            pltpu.VMEM((ts, D_MODEL), F32),
        ],
        compiler_params=_params(("parallel", "arbitrary")),
        name="conv_branch",
    )(z, z, w_dw, b_dw, ln_g, ln_b)


def _merge_kernel(x_ref, c_ref, a_ref, u_ref, uh_ref, ng_ref, wg_ref, pw_ref, ao_ref,
                  plw_ref, pls_ref, plo_ref, wo_ref, o_ref, ext_ref, pp_ref):
    ts = SEQ_TS
    i = pl.program_id(1)
    x = x_ref[...]
    h = _rms(x, ng_ref[2:3, :]).astype(BF16)

    ext_ref[0:HALO, :] = jnp.where(i == 0, 0.0, uh_ref[...])
    ext_ref[HALO:HALO + ts, :] = u_ref[...]
    pos = i * ts + lax.broadcasted_iota(jnp.int32, (ts, 1), 0)
    for g, w in enumerate(POOL_WINDOWS):
        ls = slice(g * POOL_GD, (g + 1) * POOL_GD)
        u = ext_ref[HALO:HALO + ts, ls]
        tot = u
        for j in range(1, w):
            tot = tot + ext_ref[HALO - j:HALO - j + ts, ls]
        cnt = jnp.minimum(pos + 1, w).astype(F32)
        pooled = (tot / cnt - u).astype(BF16)
        yg = _dot(pooled, plw_ref[g]) * pls_ref[:, ls]
        pp_ref[:, ls] = yg.astype(BF16)

    y_pool = _dot(pp_ref[...], plo_ref[...])
    y_conv = _dot(c_ref[...], pw_ref[...])
    y_attn = _dot(a_ref[...], ao_ref[...])
    merged = jax.nn.sigmoid(_dot(h, wg_ref[:, 0:D_MODEL])) * y_conv
    merged = merged + jax.nn.sigmoid(_dot(h, wg_ref[:, D_MODEL:2 * D_MODEL])) * y_attn
    merged = merged + jax.nn.sigmoid(_dot(h, wg_ref[:, 2 * D_MODEL:3 * D_MODEL])) * y_pool
    y = _dot(merged.astype(BF16), wo_ref[...])
    o_ref[...] = x + _rms(y, ng_ref[3:4, :])


def _merge(x, c_act, attn, u_pool, ng, w_in, conv_pw, attn_o, pool_w, pool_scale, pool_o, w_out, l):
    b, s, _ = x.shape
    ts = SEQ_TS
    tile = pl.BlockSpec((None, ts, D_MODEL), lambda bi, i: (bi, i, 0))
    sq = _resident((None, D_MODEL, D_MODEL), lambda bi, i: (l, 0, 0))
    return pl.pallas_call(
        _merge_kernel,
        grid=(b, s // ts),
        in_specs=[
            tile, tile, tile, tile,
            _halo_spec(ts),
            _resident((None, 8, D_MODEL), lambda bi, i: (l, 0, 0)),
            _resident((None, D_MODEL, 3 * D_MODEL), lambda bi, i: (l, 0, COL_GATES // (3 * D_MODEL))),
            sq, sq,
            _resident((None, len(POOL_WINDOWS), POOL_GD, POOL_GD), lambda bi, i: (l, 0, 0, 0)),
            _resident((1, D_MODEL), lambda bi, i: (0, 0)),
            sq, sq,
        ],
        out_specs=tile,
        out_shape=jax.ShapeDtypeStruct((b, s, D_MODEL), F32),
        scratch_shapes=[
            pltpu.VMEM((HALO + ts, D_MODEL), F32),
            pltpu.VMEM((ts, D_MODEL), BF16),
        ],
        compiler_params=_params(("parallel", "arbitrary")),
        name="merge",
    )(x, c_act, attn, u_pool, u_pool, ng, w_in, conv_pw, attn_o, pool_w, pool_scale, pool_o, w_out)


def _xkv_kernel(m_ref, g_ref, w_ref, k_ref, v_ref):
    mn = _rms(m_ref[...], g_ref[...]).astype(BF16)
    k_ref[...] = _dot(mn, w_ref[:, 0:D_MODEL]).astype(k_ref.dtype)
    v_ref[...] = _dot(mn, w_ref[:, D_MODEL:2 * D_MODEL]).astype(v_ref.dtype)


def _xattn_kv(mem, g, wkv, l):
    b, m, _ = mem.shape
    out = pl.BlockSpec((None, m, D_MODEL), lambda bi: (bi, 0, 0))
    return pl.pallas_call(
        _xkv_kernel,
        grid=(b,),
        in_specs=[
            pl.BlockSpec((None, m, D_MODEL), lambda bi: (bi, 0, 0)),
            pl.BlockSpec((1, D_MODEL), lambda bi: (0, 0)),
            _resident((None, D_MODEL, 2 * D_MODEL), lambda bi: (l, 0, 0)),
        ],
        out_specs=[out, out],
        out_shape=[jax.ShapeDtypeStruct((b, m, D_MODEL), BF16)] * 2,
        compiler_params=_params(("parallel",)),
        name="xattn_kv",
    )(mem, g, wkv)


def _xattn_kernel(x_ref, k_ref, v_ref, ng_ref, wq_ref, wo_ref, o_ref, oh_ref):
    x = x_ref[...]
    h = _rms(x, ng_ref[4:5, :]).astype(BF16)
    q = (_dot(h, wq_ref[...]) * (X_HEAD_DIM ** -0.5)).astype(BF16)
    for hd in range(X_HEADS):
        ls = slice(hd * X_HEAD_DIM, (hd + 1) * X_HEAD_DIM)
        s = lax.dot_general(q[:, ls], k_ref[:, ls], (((1,), (1,)), ((), ())),
                            preferred_element_type=F32)
        p = jnp.exp(s - jnp.max(s, axis=-1, keepdims=True))
        denom = jnp.sum(p, axis=-1, keepdims=True)
        oh_ref[:, ls] = (_dot(p.astype(BF16), v_ref[:, ls]) / denom).astype(BF16)
    y = _dot(oh_ref[...], wo_ref[...])
    o_ref[...] = x + _rms(y, ng_ref[5:6, :])


def _xattn(x, k, v, ng, wq, wo, l):
    b, s, _ = x.shape
    m = k.shape[1]
    ts = SEQ_TS
    tile = pl.BlockSpec((None, ts, D_MODEL), lambda bi, i: (bi, i, 0))
    kv = pl.BlockSpec((None, m, D_MODEL), lambda bi, i: (bi, 0, 0))
    sq = _resident((None, D_MODEL, D_MODEL), lambda bi, i: (l, 0, 0))
    return pl.pallas_call(
        _xattn_kernel,
        grid=(b, s // ts),
        in_specs=[tile, kv, kv, _resident((None, 8, D_MODEL), lambda bi, i: (l, 0, 0)), sq, sq],
        out_specs=tile,
        out_shape=jax.ShapeDtypeStruct((b, s, D_MODEL), F32),
        scratch_shapes=[pltpu.VMEM((ts, D_MODEL), BF16)],
        compiler_params=_params(("parallel", "arbitrary")),
        name="xattn",
    )(x, k, v, ng, wq, wo)


def kernel(x, mem, norm_g, ffn_w1, ffn_w3, ffn_w2, w_in, conv_dw, conv_dw_b, conv_ln_g, conv_ln_b, conv_pw, attn_lam, attn_subln, attn_o, rel_bias, pool_w, pool_scale, pool_o, w_out, mem_norm, xattn_q, xattn_kv, xattn_o):
    b, s, d = x.shape
    n = b * s
    w1, w3, w2 = ffn_w1.astype(BF16), ffn_w3.astype(BF16), ffn_w2.astype(BF16)
    w_in_b = w_in.astype(BF16)
    conv_pw_b, attn_o_b, pool_w_b = conv_pw.astype(BF16), attn_o.astype(BF16), pool_w.astype(BF16)
    pool_o_b, w_out_b = pool_o.astype(BF16), w_out.astype(BF16)
    xq_b, xkv_b, xo_b = xattn_q.astype(BF16), xattn_kv.astype(BF16), xattn_o.astype(BF16)
    bias = _bias_tiles(rel_bias)

    for l in range(DEPTH):
        lam_init = 0.8 - 0.6 * math.exp(-0.3 * l)
        x2 = _ffn(x.reshape(n, d), norm_g, w1, w3, w2, l, 0)
        qkv = _qkv_proj(x2, norm_g, w_in_b, l).reshape(b, s, 3 * d)
        z, u_pool = _convpool_proj(x2, norm_g, w_in_b, l)
        attn = _diff_attention(qkv, bias, attn_lam[l], attn_subln[l][None, :], lam_init)
        c_act = _conv_branch(z.reshape(b, s, d), conv_dw[l], conv_dw_b[l][None, :],
                             conv_ln_g[l][None, :], conv_ln_b[l][None, :])
        x = _merge(x2.reshape(b, s, d), c_act, attn, u_pool.reshape(b, s, d), norm_g, w_in_b,
                   conv_pw_b, attn_o_b, pool_w_b, pool_scale[l][None, :], pool_o_b, w_out_b, l)
        xk, xv = _xattn_kv(mem, mem_norm[l][None, :], xkv_b, l)
        x = _xattn(x, xk, xv, norm_g, xq_b, xo_b, l)
        x = _ffn(x.reshape(n, d), norm_g, w1, w3, w2, l, 1).reshape(b, s, d)
    return x
```

```python
import functools
import math

import jax
import jax.numpy as jnp
from jax import lax
from jax.experimental import pallas as pl
from jax.experimental.pallas import tpu as pltpu

D_MODEL = 1024
DEPTH = 4
D_FF = 2816
CONV_K = 31
N_HEADS = 8
HEAD_DIM = 64
V_DIM = 2 * HEAD_DIM
N_BUCKETS = 32
MAX_DISTANCE = 128
POOL_WINDOWS = (2, 4, 8, 16)
POOL_GD = D_MODEL // len(POOL_WINDOWS)
X_HEADS = 4
X_HEAD_DIM = D_MODEL // X_HEADS
COL_CONV_A = 0
COL_CONV_G = D_MODEL
COL_QKV = 2 * D_MODEL
COL_POOL = 5 * D_MODEL
COL_GATES = 6 * D_MODEL
RMS_EPS = 1e-6
LN_EPS = 1e-5
NEG_INF = -1e30
LOG2E = math.log2(math.e)

F32 = jnp.float32
BF16 = jnp.bfloat16

HALO = 32
FFN_TM = 512
FFN_TF = 256
PROJ_TM = 512
SEQ_TS = 256
ATT_T = 256
VMEM_LIMIT = 56 * 1024 * 1024


def _rms(x, g):
    ms = jnp.mean(x * x, axis=-1, keepdims=True)
    return x * lax.rsqrt(ms + RMS_EPS) * g


def _dot(a, b):
    return jnp.dot(a, b, preferred_element_type=F32)


def _params(sem):
    return pltpu.CompilerParams(dimension_semantics=sem, vmem_limit_bytes=VMEM_LIMIT)


def _resident(shape, index_map):
    return pl.BlockSpec(shape, index_map, pipeline_mode=pl.Buffered(1))


def _ffn_kernel(x_ref, ng_ref, w1_ref, w3_ref, w2_ref, o_ref, *, g_in, g_out):
    x = x_ref[...]
    h = _rms(x, ng_ref[g_in:g_in + 1, :]).astype(BF16)
    acc = None
    for c in range(D_FF // FFN_TF):
        sl = slice(c * FFN_TF, (c + 1) * FFN_TF)
        a = _dot(h, w1_ref[:, sl])
        b = _dot(h, w3_ref[:, sl])
        t = (a * jax.nn.sigmoid(a) * b).astype(BF16)
        y = _dot(t, w2_ref[sl, :])
        acc = y if acc is None else acc + y
    o_ref[...] = x + 0.5 * _rms(acc, ng_ref[g_out:g_out + 1, :])


def _ffn(x2, ng, w1, w3, w2, l, s):
    n = x2.shape[0]
    g_in, g_out = (0, 1) if s == 0 else (6, 7)
    return pl.pallas_call(
        functools.partial(_ffn_kernel, g_in=g_in, g_out=g_out),
        grid=(n // FFN_TM,),
        in_specs=[
            pl.BlockSpec((FFN_TM, D_MODEL), lambda i: (i, 0)),
            _resident((None, 8, D_MODEL), lambda i: (l, 0, 0)),
            _resident((None, None, D_MODEL, D_FF), lambda i: (l, s, 0, 0)),
            _resident((None, None, D_MODEL, D_FF), lambda i: (l, s, 0, 0)),
            _resident((None, None, D_FF, D_MODEL), lambda i: (l, s, 0, 0)),
        ],
        out_specs=pl.BlockSpec((FFN_TM, D_MODEL), lambda i: (i, 0)),
        out_shape=jax.ShapeDtypeStruct((n, D_MODEL), F32),
        compiler_params=_params(("parallel",)),
        name="ffn",
    )(x2, ng, w1, w3, w2)


def _qkv_kernel(x_ref, ng_ref, w_ref, o_ref, h_ref):
    j = pl.program_id(1)

    @pl.when(j == 0)
    def _():
        h_ref[...] = _rms(x_ref[...], ng_ref[2:3, :]).astype(BF16)

    scale = jnp.where(j == 0, HEAD_DIM ** -0.5 * LOG2E, 1.0).astype(F32)
    o_ref[...] = (_dot(h_ref[...], w_ref[...]) * scale).astype(o_ref.dtype)


def _qkv_proj(x2, ng, w_in, l):
    n = x2.shape[0]
    tn = D_MODEL
    return pl.pallas_call(
        _qkv_kernel,
        grid=(n // PROJ_TM, 3),
        in_specs=[
            pl.BlockSpec((PROJ_TM, D_MODEL), lambda i, j: (i, 0)),
            _resident((None, 8, D_MODEL), lambda i, j: (l, 0, 0)),
            pl.BlockSpec((None, D_MODEL, tn), lambda i, j: (l, 0, COL_QKV // tn + j)),
        ],
        out_specs=pl.BlockSpec((PROJ_TM, tn), lambda i, j: (i, j)),
        out_shape=jax.ShapeDtypeStruct((n, 3 * D_MODEL), BF16),
        scratch_shapes=[pltpu.VMEM((PROJ_TM, D_MODEL), BF16)],
        compiler_params=_params(("parallel", "arbitrary")),
        name="qkv_proj",
    )(x2, ng, w_in)


def _convpool_proj_kernel(x_ref, ng_ref, wa_ref, wg_ref, wp_ref, z_ref, u_ref):
    h = _rms(x_ref[...], ng_ref[2:3, :]).astype(BF16)
    a = _dot(h, wa_ref[...])
    gt = _dot(h, wg_ref[...])
    z_ref[...] = a * jax.nn.sigmoid(gt)
    u_ref[...] = _dot(h, wp_ref[...])


def _convpool_proj(x2, ng, w_in, l):
    n = x2.shape[0]
    tn = D_MODEL
    row = pl.BlockSpec((PROJ_TM, D_MODEL), lambda i: (i, 0))
    return pl.pallas_call(
        _convpool_proj_kernel,
        grid=(n // PROJ_TM,),
        in_specs=[
            row,
            _resident((None, 8, D_MODEL), lambda i: (l, 0, 0)),
            _resident((None, D_MODEL, tn), lambda i: (l, 0, COL_CONV_A // tn)),
            _resident((None, D_MODEL, tn), lambda i: (l, 0, COL_CONV_G // tn)),
            _resident((None, D_MODEL, tn), lambda i: (l, 0, COL_POOL // tn)),
        ],
        out_specs=[row, row],
        out_shape=[jax.ShapeDtypeStruct((n, D_MODEL), F32)] * 2,
        compiler_params=_params(("parallel",)),
        name="convpool_proj",
    )(x2, ng, w_in, w_in, w_in)


def _bias_kernel(rb_ref, o_ref):
    h = pl.program_id(0)
    d = pl.program_id(1)
    t = ATT_T
    i = lax.broadcasted_iota(jnp.int32, (t, t), 0)
    j = lax.broadcasted_iota(jnp.int32, (t, t), 1)
    rel = d * t + i - j
    n = jnp.maximum(rel, 0)
    max_exact = N_BUCKETS // 2
    nf = jnp.maximum(n, 1).astype(F32)
    large = max_exact + (jnp.log(nf / max_exact) / math.log(MAX_DISTANCE / max_exact)
                         * (N_BUCKETS - max_exact)).astype(jnp.int32)
    large = jnp.minimum(large, N_BUCKETS - 1)
    bucket = jnp.where(n < max_exact, n, large)
    val = jnp.zeros((t, t), F32)
    for b in range(N_BUCKETS):
        val = jnp.where(bucket == b, rb_ref[b, h], val)
    val = (val - rb_ref[N_BUCKETS - 1, h]) * LOG2E
    val = jnp.where(rel >= 0, val, NEG_INF)
    o_ref[0:t, :] = val
    o_ref[t:2 * t, :] = val


def _bias_tiles(rel_bias):
    t = ATT_T
    return pl.pallas_call(
        _bias_kernel,
        grid=(N_HEADS, 2),
        in_specs=[pl.BlockSpec(memory_space=pltpu.SMEM)],
        out_specs=pl.BlockSpec((None, None, 2 * t, t), lambda h, d: (h, d, 0, 0)),
        out_shape=jax.ShapeDtypeStruct((N_HEADS, 2, 2 * t, t), F32),
        compiler_params=_params(("arbitrary", "arbitrary")),
        name="bias_tiles",
    )(rel_bias)


def _attn_kernel(q_ref, k_ref, v_ref, bias_ref, lam_ref, sg_ref, o_ref, vx_ref, *, lam_init):
    t = ATT_T
    s_len = k_ref.shape[0]
    vx_ref[:, 0:V_DIM] = v_ref[...]
    vx_ref[:, V_DIM:2 * V_DIM] = jnp.ones((s_len, V_DIM), BF16)
    lam = lam_ref[...]
    lam_full = (jnp.exp(jnp.sum(lam[0:1, :] * lam[1:2, :], axis=-1, keepdims=True))
                - jnp.exp(jnp.sum(lam[2:3, :] * lam[3:4, :], axis=-1, keepdims=True)) + lam_init)
    lane = lax.broadcasted_iota(jnp.int32, (t, V_DIM), 1)
    nt = (((1,), (1,)), ((), ()))
    for qi in range(s_len // t):
        q = q_ref[qi * t:(qi + 1) * t, :]
        zero = jnp.zeros_like(q)
        q2 = jnp.concatenate([jnp.where(lane < HEAD_DIM, q, zero),
                              jnp.where(lane >= HEAD_DIM, q, zero)], axis=0)
        parts = []
        if qi >= 2:
            n_far = (qi - 1) * t
            parts.append((0, lax.dot_general(q2, k_ref[0:n_far, :], nt, preferred_element_type=F32)))
        if qi >= 1:
            k0 = (qi - 1) * t
            parts.append((k0, lax.dot_general(q2, k_ref[k0:k0 + t, :], nt,
                                              preferred_element_type=F32) + bias_ref[1]))
        k0 = qi * t
        parts.append((k0, lax.dot_general(q2, k_ref[k0:k0 + t, :], nt,
                                          preferred_element_type=F32) + bias_ref[0]))
        m = None
        for _, sp in parts:
            mp = jnp.max(sp, axis=1, keepdims=True)
            m = mp if m is None else jnp.maximum(m, mp)
        r = None
        for k0, sp in parts:
            p = jnp.exp2(sp - m).astype(BF16)
            rp = _dot(p, vx_ref[k0:k0 + sp.shape[1], :])
            r = rp if r is None else r + rp
        o = r[:, 0:V_DIM] / r[:, V_DIM:2 * V_DIM]
        out = o[0:t, :] - lam_full * o[t:2 * t, :]
        o_ref[qi * t:(qi + 1) * t, :] = (_rms(out, sg_ref[...]) * (1.0 - lam_init)).astype(o_ref.dtype)


def _diff_attention(qkv, bias, lam, subln, lam_init):
    b, s, _ = qkv.shape
    t = ATT_T
    return pl.pallas_call(
        functools.partial(_attn_kernel, lam_init=lam_init),
        grid=(b, N_HEADS),
        in_specs=[
            pl.BlockSpec((None, s, V_DIM), lambda bi, h: (bi, 0, h)),
            pl.BlockSpec((None, s, V_DIM), lambda bi, h: (bi, 0, N_HEADS + h)),
            pl.BlockSpec((None, s, V_DIM), lambda bi, h: (bi, 0, 2 * N_HEADS + h)),
            pl.BlockSpec((None, 2, 2 * t, t), lambda bi, h: (h, 0, 0, 0)),
            pl.BlockSpec((4, HEAD_DIM), lambda bi, h: (0, 0)),
            pl.BlockSpec((1, V_DIM), lambda bi, h: (0, 0)),
        ],
        out_specs=pl.BlockSpec((None, s, V_DIM), lambda bi, h: (bi, 0, h)),
        out_shape=jax.ShapeDtypeStruct((b, s, N_HEADS * V_DIM), BF16),
        scratch_shapes=[pltpu.VMEM((s, 2 * V_DIM), BF16)],
        compiler_params=_params(("parallel", "arbitrary")),
        name="diff_attention",
    )(qkv, qkv, qkv, bias, lam, subln)


def _conv_kernel(z_ref, zh_ref, w_ref, b_ref, lg_ref, lb_ref, o_ref, ext_ref, cv_ref):
    ts = SEQ_TS
    i = pl.program_id(1)
    ext_ref[0:HALO, :] = jnp.where(i == 0, 0.0, zh_ref[...])
    ext_ref[HALO:HALO + ts, :] = z_ref[...]
    rc = 64
    off = HALO - (CONV_K - 1)

    def lanes(lc, carry):
        l0 = pl.multiple_of(lc * 128, 128)
        for r in range(ts // rc):
            acc = None
            for k in range(CONV_K):
                term = w_ref[k:k + 1, pl.ds(l0, 128)] * ext_ref[r * rc + off + k:r * rc + off + k + rc, pl.ds(l0, 128)]
                acc = term if acc is None else acc + term
            cv_ref[r * rc:(r + 1) * rc, pl.ds(l0, 128)] = acc
        return carry

    lax.fori_loop(0, D_MODEL // 128, lanes, 0)
    c = cv_ref[...] + b_ref[...]
    mu = jnp.mean(c, axis=-1, keepdims=True)
    xc = c - mu
    y = xc * lax.rsqrt(jnp.mean(xc * xc, axis=-1, keepdims=True) + LN_EPS) * lg_ref[...] + lb_ref[...]
    o_ref[...] = (y * jax.nn.sigmoid(y)).astype(o_ref.dtype)


def _halo_spec(ts):
    return pl.BlockSpec((None, HALO, D_MODEL),
                        lambda b, i: (b, jnp.maximum(i * (ts // HALO) - 1, 0), 0))


def _conv_branch(z, w_dw, b_dw, ln_g, ln_b):
    b, s, _ = z.shape
    ts = SEQ_TS
    vec = pl.BlockSpec((1, D_MODEL), lambda bi, i: (0, 0))
    return pl.pallas_call(
        _conv_kernel,
        grid=(b, s // ts),
        in_specs=[
            pl.BlockSpec((None, ts, D_MODEL), lambda bi, i: (bi, i, 0)),
            _halo_spec(ts),
            pl.BlockSpec((CONV_K, D_MODEL), lambda bi, i: (0, 0)),
            vec, vec, vec,
        ],
        out_specs=pl.BlockSpec((None, ts, D_MODEL), lambda bi, i: (bi, i, 0)),
        out_shape=jax.ShapeDtypeStruct((b, s, D_MODEL), BF16),
        scratch_shapes=[
            pltpu.VMEM((HALO + ts, D_MODEL), F32),
            pltpu.VMEM((ts, D_MODEL), F32),
        ],
        compiler_params=_params(("parallel", "arbitrary")),
        name="conv_branch",
    )(z, z, w_dw, b_dw, ln_g, ln_b)


def _merge_kernel(x_ref, c_ref, a_ref, u_ref, uh_ref, ng_ref, wg_ref, pw_ref, ao_ref,
                  plw_ref, pls_ref, plo_ref, wo_ref, o_ref, ext_ref, pp_ref):
    ts = SEQ_TS
    i = pl.program_id(1)
    x = x_ref[...]
    h = _rms(x, ng_ref[2:3, :]).astype(BF16)

    ext_ref[0:HALO, :] = jnp.where(i == 0, 0.0, uh_ref[...])
    ext_ref[HALO:HALO + ts, :] = u_ref[...]
    pos = i * ts + lax.broadcasted_iota(jnp.int32, (ts, 1), 0)
    for g, w in enumerate(POOL_WINDOWS):
        ls = slice(g * POOL_GD, (g + 1) * POOL_GD)
        u = ext_ref[HALO:HALO + ts, ls]
        tot = u
        for j in range(1, w):
            tot = tot + ext_ref[HALO - j:HALO - j + ts, ls]
        cnt = jnp.minimum(pos + 1, w).astype(F32)
        pooled = (tot / cnt - u).astype(BF16)
        yg = _dot(pooled, plw_ref[g]) * pls_ref[:, ls]
        pp_ref[:, ls] = yg.astype(BF16)

    y_pool = _dot(pp_ref[...], plo_ref[...])
    y_conv = _dot(c_ref[...], pw_ref[...])
    y_attn = _dot(a_ref[...], ao_ref[...])
    merged = jax.nn.sigmoid(_dot(h, wg_ref[:, 0:D_MODEL])) * y_conv
    merged = merged + jax.nn.sigmoid(_dot(h, wg_ref[:, D_MODEL:2 * D_MODEL])) * y_attn
    merged = merged + jax.nn.sigmoid(_dot(h, wg_ref[:, 2 * D_MODEL:3 * D_MODEL])) * y_pool
    y = _dot(merged.astype(BF16), wo_ref[...])
    o_ref[...] = x + _rms(y, ng_ref[3:4, :])


def _merge(x, c_act, attn, u_pool, ng, w_in, conv_pw, attn_o, pool_w, pool_scale, pool_o, w_out, l):
    b, s, _ = x.shape
    ts = SEQ_TS
    tile = pl.BlockSpec((None, ts, D_MODEL), lambda bi, i: (bi, i, 0))
    sq = _resident((None, D_MODEL, D_MODEL), lambda bi, i: (l, 0, 0))
    return pl.pallas_call(
        _merge_kernel,
        grid=(b, s // ts),
        in_specs=[
            tile, tile, tile, tile,
            _halo_spec(ts),
            _resident((None, 8, D_MODEL), lambda bi, i: (l, 0, 0)),
            _resident((None, D_MODEL, 3 * D_MODEL), lambda bi, i: (l, 0, COL_GATES // (3 * D_MODEL))),
            sq, sq,
            _resident((None, len(POOL_WINDOWS), POOL_GD, POOL_GD), lambda bi, i: (l, 0, 0, 0)),
            _resident((1, D_MODEL), lambda bi, i: (0, 0)),
            sq, sq,
        ],
        out_specs=tile,
        out_shape=jax.ShapeDtypeStruct((b, s, D_MODEL), F32),
        scratch_shapes=[
            pltpu.VMEM((HALO + ts, D_MODEL), F32),
            pltpu.VMEM((ts, D_MODEL), BF16),
        ],
        compiler_params=_params(("parallel", "arbitrary")),
        name="merge",
    )(x, c_act, attn, u_pool, u_pool, ng, w_in, conv_pw, attn_o, pool_w, pool_scale, pool_o, w_out)


def _xkv_kernel(m_ref, g_ref, w_ref, k_ref, v_ref):
    mn = _rms(m_ref[...], g_ref[...]).astype(BF16)
    k_ref[...] = _dot(mn, w_ref[:, 0:D_MODEL]).astype(k_ref.dtype)
    v_ref[...] = _dot(mn, w_ref[:, D_MODEL:2 * D_MODEL]).astype(v_ref.dtype)


def _xattn_kv(mem, g, wkv, l):
    b, m, _ = mem.shape
    out = pl.BlockSpec((None, m, D_MODEL), lambda bi: (bi, 0, 0))
    return pl.pallas_call(
        _xkv_kernel,
        grid=(b,),
        in_specs=[
            pl.BlockSpec((None, m, D_MODEL), lambda bi: (bi, 0, 0)),
            pl.BlockSpec((1, D_MODEL), lambda bi: (0, 0)),
            _resident((None, D_MODEL, 2 * D_MODEL), lambda bi: (l, 0, 0)),
        ],
        out_specs=[out, out],
        out_shape=[jax.ShapeDtypeStruct((b, m, D_MODEL), BF16)] * 2,
        compiler_params=_params(("parallel",)),
        name="xattn_kv",
    )(mem, g, wkv)


def _xattn_kernel(x_ref, k_ref, v_ref, ng_ref, wq_ref, wo_ref, o_ref, oh_ref):
    x = x_ref[...]
    h = _rms(x, ng_ref[4:5, :]).astype(BF16)
    q = (_dot(h, wq_ref[...]) * (X_HEAD_DIM ** -0.5)).astype(BF16)
    for hd in range(X_HEADS):
        ls = slice(hd * X_HEAD_DIM, (hd + 1) * X_HEAD_DIM)
        s = lax.dot_general(q[:, ls], k_ref[:, ls], (((1,), (1,)), ((), ())),
                            preferred_element_type=F32)
        p = jnp.exp(s - jnp.max(s, axis=-1, keepdims=True))
        denom = jnp.sum(p, axis=-1, keepdims=True)
        oh_ref[:, ls] = (_dot(p.astype(BF16), v_ref[:, ls]) / denom).astype(BF16)
    y = _dot(oh_ref[...], wo_ref[...])
    o_ref[...] = x + _rms(y, ng_ref[5:6, :])


def _xattn(x, k, v, ng, wq, wo, l):
    b, s, _ = x.shape
    m = k.shape[1]
    ts = SEQ_TS
    tile = pl.BlockSpec((None, ts, D_MODEL), lambda bi, i: (bi, i, 0))
    kv = pl.BlockSpec((None, m, D_MODEL), lambda bi, i: (bi, 0, 0))
    sq = _resident((None, D_MODEL, D_MODEL), lambda bi, i: (l, 0, 0))
    return pl.pallas_call(
        _xattn_kernel,
        grid=(b, s // ts),
        in_specs=[tile, kv, kv, _resident((None, 8, D_MODEL), lambda bi, i: (l, 0, 0)), sq, sq],
        out_specs=tile,
        out_shape=jax.ShapeDtypeStruct((b, s, D_MODEL), F32),
        scratch_shapes=[pltpu.VMEM((ts, D_MODEL), BF16)],
        compiler_params=_params(("parallel", "arbitrary")),
        name="xattn",
    )(x, k, v, ng, wq, wo)


def kernel(x, mem, norm_g, ffn_w1, ffn_w3, ffn_w2, w_in, conv_dw, conv_dw_b, conv_ln_g, conv_ln_b, conv_pw, attn_lam, attn_subln, attn_o, rel_bias, pool_w, pool_scale, pool_o, w_out, mem_norm, xattn_q, xattn_kv, xattn_o):
    b, s, d = x.shape
    n = b * s
    w1, w3, w2 = ffn_w1.astype(BF16), ffn_w3.astype(BF16), ffn_w2.astype(BF16)
    w_in_b = w_in.astype(BF16)
    conv_pw_b, attn_o_b, pool_w_b = conv_pw.astype(BF16), attn_o.astype(BF16), pool_w.astype(BF16)
    pool_o_b, w_out_b = pool_o.astype(BF16), w_out.astype(BF16)
    xq_b, xkv_b, xo_b = xattn_q.astype(BF16), xattn_kv.astype(BF16), xattn_o.astype(BF16)
    bias = _bias_tiles(rel_bias)

    for l in range(DEPTH):
        lam_init = 0.8 - 0.6 * math.exp(-0.3 * l)
        x2 = _ffn(x.reshape(n, d), norm_g, w1, w3, w2, l, 0)
        qkv = _qkv_proj(x2, norm_g, w_in_b, l).reshape(b, s, 3 * d)
        z, u_pool = _convpool_proj(x2, norm_g, w_in_b, l)
        attn = _diff_attention(qkv, bias, attn_lam[l], attn_subln[l][None, :], lam_init)
        c_act = _conv_branch(z.reshape(b, s, d), conv_dw[l], conv_dw_b[l][None, :],
                             conv_ln_g[l][None, :], conv_ln_b[l][None, :])
        x = _merge(x2.reshape(b, s, d), c_act, attn, u_pool.reshape(b, s, d), norm_g, w_in_b,
                   conv_pw_b, attn_o_b, pool_w_b, pool_scale[l][None, :], pool_o_b, w_out_b, l)
        xk, xv = _xattn_kv(mem, mem_norm[l][None, :], xkv_b, l)
        x = _xattn(x, xk, xv, norm_g, xq_b, xo_b, l)
        x = _ffn(x.reshape(n, d), norm_g, w1, w3, w2, l, 1).reshape(b, s, d)
    return x
```

```python
import functools
import math

import jax
import jax.numpy as jnp
from jax import lax
from jax.experimental import pallas as pl
from jax.experimental.pallas import tpu as pltpu

D_MODEL = 1024
DEPTH = 4
D_FF = 2816
CONV_K = 31
N_HEADS = 8
HEAD_DIM = 64
V_DIM = 2 * HEAD_DIM
N_BUCKETS = 32
MAX_DISTANCE = 128
POOL_WINDOWS = (2, 4, 8, 16)
POOL_GD = D_MODEL // len(POOL_WINDOWS)
X_HEADS = 4
X_HEAD_DIM = D_MODEL // X_HEADS
COL_CONV_A = 0
COL_CONV_G = D_MODEL
COL_QKV = 2 * D_MODEL
COL_POOL = 5 * D_MODEL
COL_GATES = 6 * D_MODEL
RMS_EPS = 1e-6
LN_EPS = 1e-5
NEG_INF = -1e30
LOG2E = math.log2(math.e)

F32 = jnp.float32
BF16 = jnp.bfloat16

HALO = 32
FFN_TM = 512
FFN_TF = 256
PROJ_TM = 512
SEQ_TS = 256
ATT_T = 256
VMEM_LIMIT = 56 * 1024 * 1024


def _rms(x, g):
    ms = jnp.mean(x * x, axis=-1, keepdims=True)
    return x * lax.rsqrt(ms + RMS_EPS) * g


def _dot(a, b):
    return jnp.dot(a, b, preferred_element_type=F32)


def _params(sem):
    return pltpu.CompilerParams(dimension_semantics=sem, vmem_limit_bytes=VMEM_LIMIT)


def _resident(shape, index_map):
    return pl.BlockSpec(shape, index_map, pipeline_mode=pl.Buffered(1))


def _ffn_kernel(x_ref, ng_ref, w1_ref, w3_ref, w2_ref, o_ref, *, g_in, g_out):
    x = x_ref[...]
    h = _rms(x, ng_ref[g_in:g_in + 1, :]).astype(BF16)
    acc = None
    for c in range(D_FF // FFN_TF):
        sl = slice(c * FFN_TF, (c + 1) * FFN_TF)
        a = _dot(h, w1_ref[:, sl])
        b = _dot(h, w3_ref[:, sl])
        t = (a * jax.nn.sigmoid(a) * b).astype(BF16)
        y = _dot(t, w2_ref[sl, :])
        acc = y if acc is None else acc + y
    o_ref[...] = x + 0.5 * _rms(acc, ng_ref[g_out:g_out + 1, :])


def _ffn(x2, ng, w1, w3, w2, l, s):
    n = x2.shape[0]
    g_in, g_out = (0, 1) if s == 0 else (6, 7)
    return pl.pallas_call(
        functools.partial(_ffn_kernel, g_in=g_in, g_out=g_out),
        grid=(n // FFN_TM,),
        in_specs=[
            pl.BlockSpec((FFN_TM, D_MODEL), lambda i: (i, 0)),
            _resident((None, 8, D_MODEL), lambda i: (l, 0, 0)),
            _resident((None, None, D_MODEL, D_FF), lambda i: (l, s, 0, 0)),
            _resident((None, None, D_MODEL, D_FF), lambda i: (l, s, 0, 0)),
            _resident((None, None, D_FF, D_MODEL), lambda i: (l, s, 0, 0)),
        ],
        out_specs=pl.BlockSpec((FFN_TM, D_MODEL), lambda i: (i, 0)),
        out_shape=jax.ShapeDtypeStruct((n, D_MODEL), F32),
        compiler_params=_params(("parallel",)),
        name="ffn",
    )(x2, ng, w1, w3, w2)


def _in_proj_kernel(x_ref, ng_ref, w_ref, z_ref, qkv_ref, u_ref):
    d = D_MODEL
    h = _rms(x_ref[...], ng_ref[2:3, :]).astype(BF16)
    a = _dot(h, w_ref[:, COL_CONV_A:COL_CONV_A + d])
    gt = _dot(h, w_ref[:, COL_CONV_G:COL_CONV_G + d])
    z_ref[...] = a * jax.nn.sigmoid(gt)
    q = _dot(h, w_ref[:, COL_QKV:COL_QKV + d]) * (HEAD_DIM ** -0.5 * LOG2E)
    qkv_ref[:, 0:d] = q.astype(BF16)
    qkv_ref[:, d:2 * d] = _dot(h, w_ref[:, COL_QKV + d:COL_QKV + 2 * d]).astype(BF16)
    qkv_ref[:, 2 * d:3 * d] = _dot(h, w_ref[:, COL_QKV + 2 * d:COL_QKV + 3 * d]).astype(BF16)
    u_ref[...] = _dot(h, w_ref[:, COL_POOL:COL_POOL + d])


def _in_proj(x2, ng, w_in, l):
    n = x2.shape[0]
    row = pl.BlockSpec((PROJ_TM, D_MODEL), lambda i: (i, 0))
    return pl.pallas_call(
        _in_proj_kernel,
        grid=(n // PROJ_TM,),
        in_specs=[
            row,
            _resident((None, 8, D_MODEL), lambda i: (l, 0, 0)),
            _resident((None, D_MODEL, COL_GATES), lambda i: (l, 0, 0)),
        ],
        out_specs=[row, pl.BlockSpec((PROJ_TM, 3 * D_MODEL), lambda i: (i, 0)), row],
        out_shape=[jax.ShapeDtypeStruct((n, D_MODEL), F32),
                   jax.ShapeDtypeStruct((n, 3 * D_MODEL), BF16),
                   jax.ShapeDtypeStruct((n, D_MODEL), F32)],
        compiler_params=_params(("parallel",)),
        name="in_proj",
    )(x2, ng, w_in)


def _bias_kernel(rb_ref, o_ref):
    h = pl.program_id(0)
    d = pl.program_id(1)
    t = ATT_T
    i = lax.broadcasted_iota(jnp.int32, (t, t), 0)
    j = lax.broadcasted_iota(jnp.int32, (t, t), 1)
    rel = d * t + i - j
    n = jnp.maximum(rel, 0)
    max_exact = N_BUCKETS // 2
    nf = jnp.maximum(n, 1).astype(F32)
    large = max_exact + (jnp.log(nf / max_exact) / math.log(MAX_DISTANCE / max_exact)
                         * (N_BUCKETS - max_exact)).astype(jnp.int32)
    large = jnp.minimum(large, N_BUCKETS - 1)
    bucket = jnp.where(n < max_exact, n, large)
    val = jnp.zeros((t, t), F32)
    for b in range(N_BUCKETS):
        val = jnp.where(bucket == b, rb_ref[b, h], val)
    val = (val - rb_ref[N_BUCKETS - 1, h]) * LOG2E
    val = jnp.where(rel >= 0, val, NEG_INF)
    o_ref[0:t, :] = val
    o_ref[t:2 * t, :] = val


def _bias_tiles(rel_bias):
    t = ATT_T
    return pl.pallas_call(
        _bias_kernel,
        grid=(N_HEADS, 2),
        in_specs=[pl.BlockSpec(memory_space=pltpu.SMEM)],
        out_specs=pl.BlockSpec((None, None, 2 * t, t), lambda h, d: (h, d, 0, 0)),
        out_shape=jax.ShapeDtypeStruct((N_HEADS, 2, 2 * t, t), F32),
        compiler_params=_params(("arbitrary", "arbitrary")),
        name="bias_tiles",
    )(rel_bias)


def _attn_kernel(q_ref, k_ref, v_ref, bias_ref, lam_ref, sg_ref, o_ref, vx_ref, *, lam_init):
    t = ATT_T
    s_len = k_ref.shape[0]
    vx_ref[:, 0:V_DIM] = v_ref[...]
    vx_ref[:, V_DIM:2 * V_DIM] = jnp.ones((s_len, V_DIM), BF16)
    lam = lam_ref[...]
    lam_full = (jnp.exp(jnp.sum(lam[0:1, :] * lam[1:2, :], axis=-1, keepdims=True))
                - jnp.exp(jnp.sum(lam[2:3, :] * lam[3:4, :], axis=-1, keepdims=True)) + lam_init)
    lane = lax.broadcasted_iota(jnp.int32, (t, V_DIM), 1)
    nt = (((1,), (1,)), ((), ()))
    for qi in range(s_len // t):
        q = q_ref[qi * t:(qi + 1) * t, :]
        zero = jnp.zeros_like(q)
        q2 = jnp.concatenate([jnp.where(lane < HEAD_DIM, q, zero),
                              jnp.where(lane >= HEAD_DIM, q, zero)], axis=0)
        parts = []
        if qi >= 2:
            n_far = (qi - 1) * t
            parts.append((0, lax.dot_general(q2, k_ref[0:n_far, :], nt, preferred_element_type=F32)))
        if qi >= 1:
            k0 = (qi - 1) * t
            parts.append((k0, lax.dot_general(q2, k_ref[k0:k0 + t, :], nt,
                                              preferred_element_type=F32) + bias_ref[1]))
        k0 = qi * t
        parts.append((k0, lax.dot_general(q2, k_ref[k0:k0 + t, :], nt,
                                          preferred_element_type=F32) + bias_ref[0]))
        m = None
        for _, sp in parts:
            mp = jnp.max(sp, axis=1, keepdims=True)
            m = mp if m is None else jnp.maximum(m, mp)
        r = None
        for k0, sp in parts:
            p = jnp.exp2(sp - m).astype(BF16)
            rp = _dot(p, vx_ref[k0:k0 + sp.shape[1], :])
            r = rp if r is None else r + rp
        o = r[:, 0:V_DIM] / r[:, V_DIM:2 * V_DIM]
        out = o[0:t, :] - lam_full * o[t:2 * t, :]
        o_ref[qi * t:(qi + 1) * t, :] = (_rms(out, sg_ref[...]) * (1.0 - lam_init)).astype(o_ref.dtype)


def _diff_attention(qkv, bias, lam, subln, lam_init):
    b, s, _ = qkv.shape
    t = ATT_T
    return pl.pallas_call(
        functools.partial(_attn_kernel, lam_init=lam_init),
        grid=(b, N_HEADS),
        in_specs=[
            pl.BlockSpec((None, s, V_DIM), lambda bi, h: (bi, 0, h)),
            pl.BlockSpec((None, s, V_DIM), lambda bi, h: (bi, 0, N_HEADS + h)),
            pl.BlockSpec((None, s, V_DIM), lambda bi, h: (bi, 0, 2 * N_HEADS + h)),
            pl.BlockSpec((None, 2, 2 * t, t), lambda bi, h: (h, 0, 0, 0)),
            pl.BlockSpec((4, HEAD_DIM), lambda bi, h: (0, 0)),
            pl.BlockSpec((1, V_DIM), lambda bi, h: (0, 0)),
        ],
        out_specs=pl.BlockSpec((None, s, V_DIM), lambda bi, h: (bi, 0, h)),
        out_shape=jax.ShapeDtypeStruct((b, s, N_HEADS * V_DIM), BF16),
        scratch_shapes=[pltpu.VMEM((s, 2 * V_DIM), BF16)],
        compiler_params=_params(("parallel", "arbitrary")),
        name="diff_attention",
    )(qkv, qkv, qkv, bias, lam, subln)


def _conv_kernel(z_ref, zh_ref, w_ref, b_ref, lg_ref, lb_ref, o_ref, sh_ref, cv_ref):
    ts = SEQ_TS
    i = pl.program_id(1)
    ext = HALO + ts
    sh_ref[0, 0:HALO, :] = jnp.where(i == 0, 0.0, zh_ref[...])
    sh_ref[0, HALO:ext, :] = z_ref[...]
    rc = 64
    off = HALO - (CONV_K - 1)

    def lanes(lc, carry):
        ls = pl.ds(pl.multiple_of(lc * 128, 128), 128)
        for r in range(1, 8):
            sh_ref[r, 0:ext - 8, ls] = sh_ref[0, r:r + ext - 8, ls]
        for c in range(ts // rc):
            acc = None
            for k in range(CONV_K):
                a, r = divmod(off + k, 8)
                row0 = c * rc + 8 * a
                term = w_ref[k:k + 1, ls] * sh_ref[r, row0:row0 + rc, ls]
                acc = term if acc is None else acc + term
            cv_ref[c * rc:(c + 1) * rc, ls] = acc
        return carry

    lax.fori_loop(0, D_MODEL // 128, lanes, 0)
    c = cv_ref[...] + b_ref[...]
    mu = jnp.mean(c, axis=-1, keepdims=True)
    xc = c - mu
    y = xc * lax.rsqrt(jnp.mean(xc * xc, axis=-1, keepdims=True) + LN_EPS) * lg_ref[...] + lb_ref[...]
    o_ref[...] = (y * jax.nn.sigmoid(y)).astype(o_ref.dtype)


def _halo_spec(ts):
    return pl.BlockSpec((None, HALO, D_MODEL),
                        lambda b, i: (b, jnp.maximum(i * (ts // HALO) - 1, 0), 0))


def _conv_branch(z, w_dw, b_dw, ln_g, ln_b):
    b, s, _ = z.shape
    ts = SEQ_TS
    vec = pl.BlockSpec((1, D_MODEL), lambda bi, i: (0, 0))
    return pl.pallas_call(
        _conv_kernel,
        grid=(b, s // ts),
        in_specs=[
            pl.BlockSpec((None, ts, D_MODEL), lambda bi, i: (bi, i, 0)),
            _halo_spec(ts),
            pl.BlockSpec((CONV_K, D_MODEL), lambda bi, i: (0, 0)),
            vec, vec, vec,
        ],
        out_specs=pl.BlockSpec((None, ts, D_MODEL), lambda bi, i: (bi, i, 0)),
        out_shape=jax.ShapeDtypeStruct((b, s, D_MODEL), BF16),
        scratch_shapes=[
            pltpu.VMEM((8, HALO + ts, D_MODEL), F32),
            pltpu.VMEM((ts, D_MODEL), F32),
        ],
        compiler_params=_params(("parallel", "arbitrary")),
        name="conv_branch",
    )(z, z, w_dw, b_dw, ln_g, ln_b)


def _merge_kernel(x_ref, c_ref, a_ref, u_ref, uh_ref, ng_ref, wg_ref, pw_ref, ao_ref,
                  plw_ref, pls_ref, plo_ref, wo_ref, o_ref, ext_ref, pp_ref):
    ts = SEQ_TS
    i = pl.program_id(1)
    x = x_ref[...]
    h = _rms(x, ng_ref[2:3, :]).astype(BF16)

    ext_ref[0:HALO, :] = jnp.where(i == 0, 0.0, uh_ref[...])
    ext_ref[HALO:HALO + ts, :] = u_ref[...]
    pos = i * ts + lax.broadcasted_iota(jnp.int32, (ts, 1), 0)
    for g, w in enumerate(POOL_WINDOWS):
        ls = slice(g * POOL_GD, (g + 1) * POOL_GD)
        u = ext_ref[HALO:HALO + ts, ls]
        tot = u
        for j in range(1, w):
            tot = tot + ext_ref[HALO - j:HALO - j + ts, ls]
        cnt = jnp.minimum(pos + 1, w).astype(F32)
        pooled = (tot / cnt - u).astype(BF16)
        yg = _dot(pooled, plw_ref[g]) * pls_ref[:, ls]
        pp_ref[:, ls] = yg.astype(BF16)

    y_pool = _dot(pp_ref[...], plo_ref[...])
    y_conv = _dot(c_ref[...], pw_ref[...])
    y_attn = _dot(a_ref[...], ao_ref[...])
    merged = jax.nn.sigmoid(_dot(h, wg_ref[:, 0:D_MODEL])) * y_conv
    merged = merged + jax.nn.sigmoid(_dot(h, wg_ref[:, D_MODEL:2 * D_MODEL])) * y_attn
    merged = merged + jax.nn.sigmoid(_dot(h, wg_ref[:, 2 * D_MODEL:3 * D_MODEL])) * y_pool
    y = _dot(merged.astype(BF16), wo_ref[...])
    o_ref[...] = x + _rms(y, ng_ref[3:4, :])


def _merge(x, c_act, attn, u_pool, ng, w_in, conv_pw, attn_o, pool_w, pool_scale, pool_o, w_out, l):
    b, s, _ = x.shape
    ts = SEQ_TS
    tile = pl.BlockSpec((None, ts, D_MODEL), lambda bi, i: (bi, i, 0))
    sq = _resident((None, D_MODEL, D_MODEL), lambda bi, i: (l, 0, 0))
    return pl.pallas_call(
        _merge_kernel,
        grid=(b, s // ts),
        in_specs=[
            tile, tile, tile, tile,
            _halo_spec(ts),
            _resident((None, 8, D_MODEL), lambda bi, i: (l, 0, 0)),
            _resident((None, D_MODEL, 3 * D_MODEL), lambda bi, i: (l, 0, COL_GATES // (3 * D_MODEL))),
            sq, sq,
            _resident((None, len(POOL_WINDOWS), POOL_GD, POOL_GD), lambda bi, i: (l, 0, 0, 0)),
            _resident((1, D_MODEL), lambda bi, i: (0, 0)),
            sq, sq,
        ],
        out_specs=tile,
        out_shape=jax.ShapeDtypeStruct((b, s, D_MODEL), F32),
        scratch_shapes=[
            pltpu.VMEM((HALO + ts, D_MODEL), F32),
            pltpu.VMEM((ts, D_MODEL), BF16),
        ],
        compiler_params=_params(("parallel", "arbitrary")),
        name="merge",
    )(x, c_act, attn, u_pool, u_pool, ng, w_in, conv_pw, attn_o, pool_w, pool_scale, pool_o, w_out)


def _xkv_kernel(m_ref, g_ref, w_ref, k_ref, v_ref):
    mn = _rms(m_ref[...], g_ref[...]).astype(BF16)
    k_ref[...] = _dot(mn, w_ref[:, 0:D_MODEL]).astype(k_ref.dtype)
    v_ref[...] = _dot(mn, w_ref[:, D_MODEL:2 * D_MODEL]).astype(v_ref.dtype)


def _xattn_kv(mem, g, wkv, l):
    b, m, _ = mem.shape
    out = pl.BlockSpec((None, m, D_MODEL), lambda bi: (bi, 0, 0))
    return pl.pallas_call(
        _xkv_kernel,
        grid=(b,),
        in_specs=[
            pl.BlockSpec((None, m, D_MODEL), lambda bi: (bi, 0, 0)),
            pl.BlockSpec((1, D_MODEL), lambda bi: (0, 0)),
            _resident((None, D_MODEL, 2 * D_MODEL), lambda bi: (l, 0, 0)),
        ],
        out_specs=[out, out],
        out_shape=[jax.ShapeDtypeStruct((b, m, D_MODEL), BF16)] * 2,
        compiler_params=_params(("parallel",)),
        name="xattn_kv",
    )(mem, g, wkv)


def _xattn_kernel(x_ref, k_ref, v_ref, ng_ref, wq_ref, wo_ref, o_ref, oh_ref):
    x = x_ref[...]
    h = _rms(x, ng_ref[4:5, :]).astype(BF16)
    q = (_dot(h, wq_ref[...]) * (X_HEAD_DIM ** -0.5)).astype(BF16)
    for hd in range(X_HEADS):
        ls = slice(hd * X_HEAD_DIM, (hd + 1) * X_HEAD_DIM)
        s = lax.dot_general(q[:, ls], k_ref[:, ls], (((1,), (1,)), ((), ())),
                            preferred_element_type=F32)
        p = jnp.exp(s - jnp.max(s, axis=-1, keepdims=True))
        denom = jnp.sum(p, axis=-1, keepdims=True)
        oh_ref[:, ls] = (_dot(p.astype(BF16), v_ref[:, ls]) / denom).astype(BF16)
    y = _dot(oh_ref[...], wo_ref[...])
    o_ref[...] = x + _rms(y, ng_ref[5:6, :])


def _xattn(x, k, v, ng, wq, wo, l):
    b, s, _ = x.shape
    m = k.shape[1]
    ts = SEQ_TS
    tile = pl.BlockSpec((None, ts, D_MODEL), lambda bi, i: (bi, i, 0))
    kv = pl.BlockSpec((None, m, D_MODEL), lambda bi, i: (bi, 0, 0))
    sq = _resident((None, D_MODEL, D_MODEL), lambda bi, i: (l, 0, 0))
    return pl.pallas_call(
        _xattn_kernel,
        grid=(b, s // ts),
        in_specs=[tile, kv, kv, _resident((None, 8, D_MODEL), lambda bi, i: (l, 0, 0)), sq, sq],
        out_specs=tile,
        out_shape=jax.ShapeDtypeStruct((b, s, D_MODEL), F32),
        scratch_shapes=[pltpu.VMEM((ts, D_MODEL), BF16)],
        compiler_params=_params(("parallel", "arbitrary")),
        name="xattn",
    )(x, k, v, ng, wq, wo)


def kernel(x, mem, norm_g, ffn_w1, ffn_w3, ffn_w2, w_in, conv_dw, conv_dw_b, conv_ln_g, conv_ln_b, conv_pw, attn_lam, attn_subln, attn_o, rel_bias, pool_w, pool_scale, pool_o, w_out, mem_norm, xattn_q, xattn_kv, xattn_o):
    b, s, d = x.shape
    n = b * s
    w1, w3, w2 = ffn_w1.astype(BF16), ffn_w3.astype(BF16), ffn_w2.astype(BF16)
    w_in_b = w_in.astype(BF16)
    conv_pw_b, attn_o_b, pool_w_b = conv_pw.astype(BF16), attn_o.astype(BF16), pool_w.astype(BF16)
    pool_o_b, w_out_b = pool_o.astype(BF16), w_out.astype(BF16)
    xq_b, xkv_b, xo_b = xattn_q.astype(BF16), xattn_kv.astype(BF16), xattn_o.astype(BF16)
    bias = _bias_tiles(rel_bias)

    for l in range(DEPTH):
        lam_init = 0.8 - 0.6 * math.exp(-0.3 * l)
        x2 = _ffn(x.reshape(n, d), norm_g, w1, w3, w2, l, 0)
        z, qkv, u_pool = _in_proj(x2, norm_g, w_in_b, l)
        attn = _diff_attention(qkv.reshape(b, s, 3 * d), bias, attn_lam[l], attn_subln[l][None, :],
                               lam_init)
        c_act = _conv_branch(z.reshape(b, s, d), conv_dw[l], conv_dw_b[l][None, :],
                             conv_ln_g[l][None, :], conv_ln_b[l][None, :])
        x = _merge(x2.reshape(b, s, d), c_act, attn, u_pool.reshape(b, s, d), norm_g, w_in_b,
                   conv_pw_b, attn_o_b, pool_w_b, pool_scale[l][None, :], pool_o_b, w_out_b, l)
        xk, xv = _xattn_kv(mem, mem_norm[l][None, :], xkv_b, l)
        x = _xattn(x, xk, xv, norm_g, xq_b, xo_b, l)
        x = _ffn(x.reshape(n, d), norm_g, w1, w3, w2, l, 1).reshape(b, s, d)
    return x
```

```python
import functools
import math

import jax
import jax.numpy as jnp
from jax import lax
from jax.experimental import pallas as pl
from jax.experimental.pallas import tpu as pltpu

D_MODEL = 1024
DEPTH = 4
D_FF = 2816
CONV_K = 31
N_HEADS = 8
HEAD_DIM = 64
V_DIM = 2 * HEAD_DIM
N_BUCKETS = 32
MAX_DISTANCE = 128
POOL_WINDOWS = (2, 4, 8, 16)
POOL_GD = D_MODEL // len(POOL_WINDOWS)
X_HEADS = 4
X_HEAD_DIM = D_MODEL // X_HEADS
COL_CONV_A = 0
COL_CONV_G = D_MODEL
COL_QKV = 2 * D_MODEL
COL_POOL = 5 * D_MODEL
COL_GATES = 6 * D_MODEL
RMS_EPS = 1e-6
LN_EPS = 1e-5
NEG_INF = -1e30
LOG2E = math.log2(math.e)

F32 = jnp.float32
BF16 = jnp.bfloat16

MXU_N = 256
HALO = 32
FFN_TM = 512
FFN_TF = 256
PROJ_TM = 512
SEQ_TS = 256
ATT_T = 256
VMEM_LIMIT = 56 * 1024 * 1024


def _rms(x, g):
    ms = jnp.mean(x * x, axis=-1, keepdims=True)
    return x * lax.rsqrt(ms + RMS_EPS) * g


def _dot(a, b):
    return jnp.dot(a, b, preferred_element_type=F32)


def _dot_cols(a, w_ref, rows=slice(None), col0=0, ncols=D_MODEL):
    return jnp.concatenate(
        [_dot(a, w_ref[rows, c:c + MXU_N]) for c in range(col0, col0 + ncols, MXU_N)], axis=1)


def _params(sem):
    return pltpu.CompilerParams(dimension_semantics=sem, vmem_limit_bytes=VMEM_LIMIT)


def _resident(shape, index_map):
    return pl.BlockSpec(shape, index_map, pipeline_mode=pl.Buffered(1))


def _ffn_kernel(x_ref, ng_ref, w1_ref, w3_ref, w2_ref, o_ref, *, g_in, g_out):
    x = x_ref[...]
    h = _rms(x, ng_ref[g_in:g_in + 1, :]).astype(BF16)
    acc = None
    for c in range(D_FF // FFN_TF):
        sl = slice(c * FFN_TF, (c + 1) * FFN_TF)
        a = _dot(h, w1_ref[:, sl])
        b = _dot(h, w3_ref[:, sl])
        t = (a * jax.nn.sigmoid(a) * b).astype(BF16)
        y = _dot_cols(t, w2_ref, rows=sl)
        acc = y if acc is None else acc + y
    o_ref[...] = x + 0.5 * _rms(acc, ng_ref[g_out:g_out + 1, :])


def _ffn(x2, ng, w1, w3, w2, l, s):
    n = x2.shape[0]
    g_in, g_out = (0, 1) if s == 0 else (6, 7)
    return pl.pallas_call(
        functools.partial(_ffn_kernel, g_in=g_in, g_out=g_out),
        grid=(n // FFN_TM,),
        in_specs=[
            pl.BlockSpec((FFN_TM, D_MODEL), lambda i: (i, 0)),
            _resident((None, 8, D_MODEL), lambda i: (l, 0, 0)),
            _resident((None, None, D_MODEL, D_FF), lambda i: (l, s, 0, 0)),
            _resident((None, None, D_MODEL, D_FF), lambda i: (l, s, 0, 0)),
            _resident((None, None, D_FF, D_MODEL), lambda i: (l, s, 0, 0)),
        ],
        out_specs=pl.BlockSpec((FFN_TM, D_MODEL), lambda i: (i, 0)),
        out_shape=jax.ShapeDtypeStruct((n, D_MODEL), F32),
        compiler_params=_params(("parallel",)),
        name="ffn",
    )(x2, ng, w1, w3, w2)


def _in_proj_kernel(x_ref, ng_ref, w_ref, z_ref, qkv_ref, u_ref):
    d = D_MODEL
    h = _rms(x_ref[...], ng_ref[2:3, :]).astype(BF16)
    a = _dot_cols(h, w_ref, col0=COL_CONV_A)
    gt = _dot_cols(h, w_ref, col0=COL_CONV_G)
    z_ref[...] = a * jax.nn.sigmoid(gt)
    q = _dot_cols(h, w_ref, col0=COL_QKV) * (HEAD_DIM ** -0.5 * LOG2E)
    qkv_ref[:, 0:d] = q.astype(BF16)
    qkv_ref[:, d:3 * d] = _dot_cols(h, w_ref, col0=COL_QKV + d, ncols=2 * d).astype(BF16)
    u_ref[...] = _dot_cols(h, w_ref, col0=COL_POOL)


def _in_proj(x2, ng, w_in, l):
    n = x2.shape[0]
    row = pl.BlockSpec((PROJ_TM, D_MODEL), lambda i: (i, 0))
    return pl.pallas_call(
        _in_proj_kernel,
        grid=(n // PROJ_TM,),
        in_specs=[
            row,
            _resident((None, 8, D_MODEL), lambda i: (l, 0, 0)),
            _resident((None, D_MODEL, COL_GATES), lambda i: (l, 0, 0)),
        ],
        out_specs=[row, pl.BlockSpec((PROJ_TM, 3 * D_MODEL), lambda i: (i, 0)), row],
        out_shape=[jax.ShapeDtypeStruct((n, D_MODEL), F32),
                   jax.ShapeDtypeStruct((n, 3 * D_MODEL), BF16),
                   jax.ShapeDtypeStruct((n, D_MODEL), F32)],
        compiler_params=_params(("parallel",)),
        name="in_proj",
    )(x2, ng, w_in)


def _bias_kernel(rb_ref, o_ref):
    h = pl.program_id(0)
    d = pl.program_id(1)
    t = ATT_T
    i = lax.broadcasted_iota(jnp.int32, (t, t), 0)
    j = lax.broadcasted_iota(jnp.int32, (t, t), 1)
    rel = d * t + i - j
    n = jnp.maximum(rel, 0)
    max_exact = N_BUCKETS // 2
    nf = jnp.maximum(n, 1).astype(F32)
    large = max_exact + (jnp.log(nf / max_exact) / math.log(MAX_DISTANCE / max_exact)
                         * (N_BUCKETS - max_exact)).astype(jnp.int32)
    large = jnp.minimum(large, N_BUCKETS - 1)
    bucket = jnp.where(n < max_exact, n, large)
    val = jnp.zeros((t, t), F32)
    for b in range(N_BUCKETS):
        val = jnp.where(bucket == b, rb_ref[b, h], val)
    val = (val - rb_ref[N_BUCKETS - 1, h]) * LOG2E
    val = jnp.where(rel >= 0, val, NEG_INF)
    o_ref[0:t, :] = val
    o_ref[t:2 * t, :] = val


def _bias_tiles(rel_bias):
    t = ATT_T
    return pl.pallas_call(
        _bias_kernel,
        grid=(N_HEADS, 2),
        in_specs=[pl.BlockSpec(memory_space=pltpu.SMEM)],
        out_specs=pl.BlockSpec((None, None, 2 * t, t), lambda h, d: (h, d, 0, 0)),
        out_shape=jax.ShapeDtypeStruct((N_HEADS, 2, 2 * t, t), F32),
        compiler_params=_params(("arbitrary", "arbitrary")),
        name="bias_tiles",
    )(rel_bias)


def _attn_kernel(q_ref, k_ref, v_ref, bias_ref, lam_ref, sg_ref, o_ref, vx_ref, *, lam_init):
    t = ATT_T
    s_len = k_ref.shape[0]
    vx_ref[:, 0:V_DIM] = v_ref[...]
    vx_ref[:, V_DIM:2 * V_DIM] = jnp.ones((s_len, V_DIM), BF16)
    lam = lam_ref[...]
    lam_full = (jnp.exp(jnp.sum(lam[0:1, :] * lam[1:2, :], axis=-1, keepdims=True))
                - jnp.exp(jnp.sum(lam[2:3, :] * lam[3:4, :], axis=-1, keepdims=True)) + lam_init)
    lane = lax.broadcasted_iota(jnp.int32, (t, V_DIM), 1)
    nt = (((1,), (1,)), ((), ()))
    for qi in reversed(range(s_len // t)):
        q = q_ref[qi * t:(qi + 1) * t, :]
        zero = jnp.zeros_like(q)
        q2 = jnp.concatenate([jnp.where(lane < HEAD_DIM, q, zero),
                              jnp.where(lane >= HEAD_DIM, q, zero)], axis=0)
        parts = []
        if qi >= 2:
            n_far = (qi - 1) * t
            parts.append((0, lax.dot_general(q2, k_ref[0:n_far, :], nt, preferred_element_type=F32)))
        if qi >= 1:
            k0 = (qi - 1) * t
            parts.append((k0, lax.dot_general(q2, k_ref[k0:k0 + t, :], nt,
                                              preferred_element_type=F32) + bias_ref[1]))
        k0 = qi * t
        parts.append((k0, lax.dot_general(q2, k_ref[k0:k0 + t, :], nt,
                                          preferred_element_type=F32) + bias_ref[0]))
        m = None
        for _, sp in parts:
            mp = jnp.max(sp, axis=1, keepdims=True)
            m = mp if m is None else jnp.maximum(m, mp)
        r = None
        for k0, sp in parts:
            p = jnp.exp2(sp - m).astype(BF16)
            rp = _dot(p, vx_ref[k0:k0 + sp.shape[1], :])
            r = rp if r is None else r + rp
        o = r[:, 0:V_DIM] / r[:, V_DIM:2 * V_DIM]
        out = o[0:t, :] - lam_full * o[t:2 * t, :]
        o_ref[qi * t:(qi + 1) * t, :] = (_rms(out, sg_ref[...]) * (1.0 - lam_init)).astype(o_ref.dtype)


def _diff_attention(qkv, bias, lam, subln, lam_init):
    b, s, _ = qkv.shape
    t = ATT_T
    return pl.pallas_call(
        functools.partial(_attn_kernel, lam_init=lam_init),
        grid=(b, N_HEADS),
        in_specs=[
            pl.BlockSpec((None, s, V_DIM), lambda bi, h: (bi, 0, h)),
            pl.BlockSpec((None, s, V_DIM), lambda bi, h: (bi, 0, N_HEADS + h)),
            pl.BlockSpec((None, s, V_DIM), lambda bi, h: (bi, 0, 2 * N_HEADS + h)),
            pl.BlockSpec((None, 2, 2 * t, t), lambda bi, h: (h, 0, 0, 0)),
            pl.BlockSpec((4, HEAD_DIM), lambda bi, h: (0, 0)),
            pl.BlockSpec((1, V_DIM), lambda bi, h: (0, 0)),
        ],
        out_specs=pl.BlockSpec((None, s, V_DIM), lambda bi, h: (bi, 0, h)),
        out_shape=jax.ShapeDtypeStruct((b, s, N_HEADS * V_DIM), BF16),
        scratch_shapes=[pltpu.VMEM((s, 2 * V_DIM), BF16)],
        compiler_params=_params(("parallel", "arbitrary")),
        name="diff_attention",
    )(qkv, qkv, qkv, bias, lam, subln)


CONV_LANES = 128
CONV_ROWS = 64


def _conv_fill(z_ref, halo, sh_ref):
    sh_ref[0, 0:HALO, :] = halo
    sh_ref[0, HALO:HALO + SEQ_TS, :] = z_ref[...]


def _conv_lanes(ls, w_ref, sh_ref, cv_ref):
    ts = SEQ_TS
    ext = HALO + ts
    off = HALO - (CONV_K - 1)
    for r in range(1, 8):
        sh_ref[r, 0:ext - 8, ls] = sh_ref[0, r:r + ext - 8, ls]
    for c in range(ts // CONV_ROWS):
        acc = None
        for k in range(CONV_K):
            a, r = divmod(off + k, 8)
            row0 = c * CONV_ROWS + 8 * a
            term = w_ref[k:k + 1, ls] * sh_ref[r, row0:row0 + CONV_ROWS, ls]
            acc = term if acc is None else acc + term
        cv_ref[c * CONV_ROWS:(c + 1) * CONV_ROWS, ls] = acc


def _conv_finish(cv_ref, b_ref, lg_ref, lb_ref):
    c = cv_ref[...] + b_ref[...]
    mu = jnp.mean(c, axis=-1, keepdims=True)
    xc = c - mu
    y = xc * lax.rsqrt(jnp.mean(xc * xc, axis=-1, keepdims=True) + LN_EPS) * lg_ref[...] + lb_ref[...]
    return (y * jax.nn.sigmoid(y)).astype(BF16)


def _merge_kernel(x_ref, z_ref, zh_ref, a_ref, u_ref, uh_ref, ng_ref, wg_ref,
                  cw_ref, cb_ref, clg_ref, clb_ref, pw_ref, ao_ref,
                  plw_ref, pls_ref, plo_ref, wo_ref, o_ref, sh_ref, cv_ref, ext_ref, pp_ref):
    ts = SEQ_TS
    i = pl.program_id(1)
    x = x_ref[...]
    h = _rms(x, ng_ref[2:3, :]).astype(BF16)

    _conv_fill(z_ref, jnp.where(i == 0, 0.0, zh_ref[...]), sh_ref)
    for c in range(0, D_MODEL, CONV_LANES):
        _conv_lanes(slice(c, c + CONV_LANES), cw_ref, sh_ref, cv_ref)
    y_conv = _dot_cols(_conv_finish(cv_ref, cb_ref, clg_ref, clb_ref), pw_ref)
    y_attn = _dot_cols(a_ref[...], ao_ref)
    merged = jax.nn.sigmoid(_dot_cols(h, wg_ref, col0=0)) * y_conv
    merged = merged + jax.nn.sigmoid(_dot_cols(h, wg_ref, col0=D_MODEL)) * y_attn

    ext_ref[0:HALO, :] = jnp.where(i == 0, 0.0, uh_ref[...])
    ext_ref[HALO:HALO + ts, :] = u_ref[...]
    pos = i * ts + lax.broadcasted_iota(jnp.int32, (ts, 1), 0)
    for g, w in enumerate(POOL_WINDOWS):
        ls = slice(g * POOL_GD, (g + 1) * POOL_GD)
        u = ext_ref[HALO:HALO + ts, ls]
        tot = u
        for j in range(1, w):
            tot = tot + ext_ref[HALO - j:HALO - j + ts, ls]
        cnt = jnp.minimum(pos + 1, w).astype(F32)
        pooled = (tot / cnt - u).astype(BF16)
        yg = _dot(pooled, plw_ref[g]) * pls_ref[:, ls]
        pp_ref[:, ls] = yg.astype(BF16)

    y_pool = _dot_cols(pp_ref[...], plo_ref)
    merged = merged + jax.nn.sigmoid(_dot_cols(h, wg_ref, col0=2 * D_MODEL)) * y_pool
    y = _dot_cols(merged.astype(BF16), wo_ref)
    o_ref[...] = x + _rms(y, ng_ref[3:4, :])


def _merge(x, z, attn, u_pool, ng, w_in, conv_dw, conv_b, conv_lg, conv_lb, conv_pw, attn_o,
           pool_w, pool_scale, pool_o, w_out, l):
    b, s, _ = x.shape
    ts = SEQ_TS
    tile = pl.BlockSpec((None, ts, D_MODEL), lambda bi, i: (bi, i, 0))
    halo = pl.BlockSpec((None, HALO, D_MODEL),
                        lambda bi, i: (bi, jnp.maximum(i * (ts // HALO) - 1, 0), 0))
    sq = _resident((None, D_MODEL, D_MODEL), lambda bi, i: (l, 0, 0))
    vec = _resident((1, D_MODEL), lambda bi, i: (0, 0))
    return pl.pallas_call(
        _merge_kernel,
        grid=(b, s // ts),
        in_specs=[
            tile, tile, halo, tile, tile, halo,
            _resident((None, 8, D_MODEL), lambda bi, i: (l, 0, 0)),
            _resident((None, D_MODEL, 3 * D_MODEL), lambda bi, i: (l, 0, COL_GATES // (3 * D_MODEL))),
            _resident((CONV_K, D_MODEL), lambda bi, i: (0, 0)),
            vec, vec, vec,
            sq, sq,
            _resident((None, len(POOL_WINDOWS), POOL_GD, POOL_GD), lambda bi, i: (l, 0, 0, 0)),
            vec,
            sq, sq,
        ],
        out_specs=tile,
        out_shape=jax.ShapeDtypeStruct((b, s, D_MODEL), F32),
        scratch_shapes=[
            pltpu.VMEM((8, HALO + ts, D_MODEL), F32),
            pltpu.VMEM((ts, D_MODEL), F32),
            pltpu.VMEM((HALO + ts, D_MODEL), F32),
            pltpu.VMEM((ts, D_MODEL), BF16),
        ],
        compiler_params=_params(("parallel", "arbitrary")),
        name="merge",
    )(x, z, z, attn, u_pool, u_pool, ng, w_in, conv_dw, conv_b, conv_lg, conv_lb, conv_pw, attn_o,
      pool_w, pool_scale, pool_o, w_out)


def _xkv_kernel(m_ref, g_ref, w_ref, k_ref, v_ref):
    mn = _rms(m_ref[...], g_ref[...]).astype(BF16)
    k_ref[...] = _dot_cols(mn, w_ref, col0=0).astype(k_ref.dtype)
    v_ref[...] = _dot_cols(mn, w_ref, col0=D_MODEL).astype(v_ref.dtype)


def _xattn_kv(mem, g, wkv, l):
    b, m, _ = mem.shape
    out = pl.BlockSpec((None, m, D_MODEL), lambda bi: (bi, 0, 0))
    return pl.pallas_call(
        _xkv_kernel,
        grid=(b,),
        in_specs=[
            pl.BlockSpec((None, m, D_MODEL), lambda bi: (bi, 0, 0)),
            pl.BlockSpec((1, D_MODEL), lambda bi: (0, 0)),
            _resident((None, D_MODEL, 2 * D_MODEL), lambda bi: (l, 0, 0)),
        ],
        out_specs=[out, out],
        out_shape=[jax.ShapeDtypeStruct((b, m, D_MODEL), BF16)] * 2,
        compiler_params=_params(("parallel",)),
        name="xattn_kv",
    )(mem, g, wkv)


def _xattn_kernel(x_ref, k_ref, v_ref, ng_ref, wq_ref, wo_ref, o_ref, oh_ref):
    x = x_ref[...]
    h = _rms(x, ng_ref[4:5, :]).astype(BF16)
    q = (_dot_cols(h, wq_ref) * (X_HEAD_DIM ** -0.5)).astype(BF16)
    for hd in range(X_HEADS):
        ls = slice(hd * X_HEAD_DIM, (hd + 1) * X_HEAD_DIM)
        s = lax.dot_general(q[:, ls], k_ref[:, ls], (((1,), (1,)), ((), ())),
                            preferred_element_type=F32)
        p = jnp.exp(s - jnp.max(s, axis=-1, keepdims=True))
        denom = jnp.sum(p, axis=-1, keepdims=True)
        oh_ref[:, ls] = (_dot(p.astype(BF16), v_ref[:, ls]) / denom).astype(BF16)
    y = _dot_cols(oh_ref[...], wo_ref)
    o_ref[...] = x + _rms(y, ng_ref[5:6, :])


def _xattn(x, k, v, ng, wq, wo, l):
    b, s, _ = x.shape
    m = k.shape[1]
    ts = SEQ_TS
    tile = pl.BlockSpec((None, ts, D_MODEL), lambda bi, i: (bi, i, 0))
    kv = pl.BlockSpec((None, m, D_MODEL), lambda bi, i: (bi, 0, 0))
    sq = _resident((None, D_MODEL, D_MODEL), lambda bi, i: (l, 0, 0))
    return pl.pallas_call(
        _xattn_kernel,
        grid=(b, s // ts),
        in_specs=[tile, kv, kv, _resident((None, 8, D_MODEL), lambda bi, i: (l, 0, 0)), sq, sq],
        out_specs=tile,
        out_shape=jax.ShapeDtypeStruct((b, s, D_MODEL), F32),
        scratch_shapes=[pltpu.VMEM((ts, D_MODEL), BF16)],
        compiler_params=_params(("parallel", "arbitrary")),
        name="xattn",
    )(x, k, v, ng, wq, wo)


def kernel(x, mem, norm_g, ffn_w1, ffn_w3, ffn_w2, w_in, conv_dw, conv_dw_b, conv_ln_g, conv_ln_b, conv_pw, attn_lam, attn_subln, attn_o, rel_bias, pool_w, pool_scale, pool_o, w_out, mem_norm, xattn_q, xattn_kv, xattn_o):
    b, s, d = x.shape
    n = b * s
    w1, w3, w2 = ffn_w1.astype(BF16), ffn_w3.astype(BF16), ffn_w2.astype(BF16)
    w_in_b = w_in.astype(BF16)
    conv_pw_b, attn_o_b, pool_w_b = conv_pw.astype(BF16), attn_o.astype(BF16), pool_w.astype(BF16)
    pool_o_b, w_out_b = pool_o.astype(BF16), w_out.astype(BF16)
    xq_b, xkv_b, xo_b = xattn_q.astype(BF16), xattn_kv.astype(BF16), xattn_o.astype(BF16)
    bias = _bias_tiles(rel_bias)

    for l in range(DEPTH):
        lam_init = 0.8 - 0.6 * math.exp(-0.3 * l)
        x2 = _ffn(x.reshape(n, d), norm_g, w1, w3, w2, l, 0)
        z, qkv, u_pool = _in_proj(x2, norm_g, w_in_b, l)
        attn = _diff_attention(qkv.reshape(b, s, 3 * d), bias, attn_lam[l], attn_subln[l][None, :],
                               lam_init)
        x = _merge(x2.reshape(b, s, d), z.reshape(b, s, d), attn, u_pool.reshape(b, s, d), norm_g,
                   w_in_b, conv_dw[l], conv_dw_b[l][None, :], conv_ln_g[l][None, :],
                   conv_ln_b[l][None, :], conv_pw_b, attn_o_b, pool_w_b, pool_scale[l][None, :],
                   pool_o_b, w_out_b, l)
        xk, xv = _xattn_kv(mem, mem_norm[l][None, :], xkv_b, l)
        x = _xattn(x, xk, xv, norm_g, xq_b, xo_b, l)
        x = _ffn(x.reshape(n, d), norm_g, w1, w3, w2, l, 1).reshape(b, s, d)
    return x
```

```python
import functools
import math

import jax
import jax.numpy as jnp
from jax import lax
from jax.experimental import pallas as pl
from jax.experimental.pallas import tpu as pltpu

D_MODEL = 1024
DEPTH = 4
D_FF = 2816
CONV_K = 31
N_HEADS = 8
HEAD_DIM = 64
V_DIM = 2 * HEAD_DIM
N_BUCKETS = 32
MAX_DISTANCE = 128
POOL_WINDOWS = (2, 4, 8, 16)
POOL_GD = D_MODEL // len(POOL_WINDOWS)
X_HEADS = 4
X_HEAD_DIM = D_MODEL // X_HEADS
COL_CONV_A = 0
COL_CONV_G = D_MODEL
COL_QKV = 2 * D_MODEL
COL_POOL = 5 * D_MODEL
COL_GATES = 6 * D_MODEL
RMS_EPS = 1e-6
LN_EPS = 1e-5
NEG_INF = -1e30
LOG2E = math.log2(math.e)

F32 = jnp.float32
BF16 = jnp.bfloat16

MXU_N = 256
SUBLANES = 8
HALO = 32
FFN_TM = 512
FFN_TF = 256
PROJ_TM = 512
SEQ_TS = 256
XATT_TS = 512
ATT_T = 256
VMEM_LIMIT = 56 * 1024 * 1024


def _rms(x, g):
    ms = jnp.mean(x * x, axis=-1, keepdims=True)
    return x * lax.rsqrt(ms + RMS_EPS) * g


def _dot(a, b):
    return jnp.dot(a, b, preferred_element_type=F32)


def _dot_cols(a, w_ref, rows=slice(None), col0=0, ncols=D_MODEL):
    return jnp.concatenate(
        [_dot(a, w_ref[rows, c:c + MXU_N]) for c in range(col0, col0 + ncols, MXU_N)], axis=1)


def _params(sem):
    return pltpu.CompilerParams(dimension_semantics=sem, vmem_limit_bytes=VMEM_LIMIT)


def _resident(shape, index_map):
    return pl.BlockSpec(shape, index_map, pipeline_mode=pl.Buffered(1))


def _ffn_kernel(x_ref, ng_ref, w1_ref, w3_ref, w2_ref, o_ref, *, g_in, g_out):
    x = x_ref[...]
    h = _rms(x, ng_ref[g_in:g_in + 1, :]).astype(BF16)
    acc = None
    for c in range(D_FF // FFN_TF):
        sl = slice(c * FFN_TF, (c + 1) * FFN_TF)
        a = _dot(h, w1_ref[:, sl])
        b = _dot(h, w3_ref[:, sl])
        t = (a * jax.nn.sigmoid(a) * b).astype(BF16)
        y = _dot_cols(t, w2_ref, rows=sl)
        acc = y if acc is None else acc + y
    o_ref[...] = x + 0.5 * _rms(acc, ng_ref[g_out:g_out + 1, :])


def _ffn(x2, ng, w1, w3, w2, l, s):
    n = x2.shape[0]
    g_in, g_out = (0, 1) if s == 0 else (6, 7)
    return pl.pallas_call(
        functools.partial(_ffn_kernel, g_in=g_in, g_out=g_out),
        grid=(n // FFN_TM,),
        in_specs=[
            pl.BlockSpec((FFN_TM, D_MODEL), lambda i: (i, 0)),
            _resident((None, 8, D_MODEL), lambda i: (l, 0, 0)),
            _resident((None, None, D_MODEL, D_FF), lambda i: (l, s, 0, 0)),
            _resident((None, None, D_MODEL, D_FF), lambda i: (l, s, 0, 0)),
            _resident((None, None, D_FF, D_MODEL), lambda i: (l, s, 0, 0)),
        ],
        out_specs=pl.BlockSpec((FFN_TM, D_MODEL), lambda i: (i, 0)),
        out_shape=jax.ShapeDtypeStruct((n, D_MODEL), F32),
        compiler_params=_params(("parallel",)),
        name="ffn",
    )(x2, ng, w1, w3, w2)


def _in_proj_kernel(x_ref, ng_ref, w_ref, z_ref, qkv_ref, u_ref):
    d = D_MODEL
    h = _rms(x_ref[...], ng_ref[2:3, :]).astype(BF16)
    a = _dot_cols(h, w_ref, col0=COL_CONV_A)
    gt = _dot_cols(h, w_ref, col0=COL_CONV_G)
    z_ref[...] = a * jax.nn.sigmoid(gt)
    q = _dot_cols(h, w_ref, col0=COL_QKV) * (HEAD_DIM ** -0.5 * LOG2E)
    qkv_ref[:, 0:d] = q.astype(BF16)
    qkv_ref[:, d:3 * d] = _dot_cols(h, w_ref, col0=COL_QKV + d, ncols=2 * d).astype(BF16)
    u_ref[...] = _dot_cols(h, w_ref, col0=COL_POOL)


def _in_proj(x2, ng, w_in, l):
    n = x2.shape[0]
    row = pl.BlockSpec((PROJ_TM, D_MODEL), lambda i: (i, 0))
    return pl.pallas_call(
        _in_proj_kernel,
        grid=(n // PROJ_TM,),
        in_specs=[
            row,
            _resident((None, 8, D_MODEL), lambda i: (l, 0, 0)),
            _resident((None, D_MODEL, COL_GATES), lambda i: (l, 0, 0)),
        ],
        out_specs=[row, pl.BlockSpec((PROJ_TM, 3 * D_MODEL), lambda i: (i, 0)), row],
        out_shape=[jax.ShapeDtypeStruct((n, D_MODEL), F32),
                   jax.ShapeDtypeStruct((n, 3 * D_MODEL), BF16),
                   jax.ShapeDtypeStruct((n, D_MODEL), F32)],
        compiler_params=_params(("parallel",)),
        name="in_proj",
    )(x2, ng, w_in)


def _bias_kernel(rb_ref, o_ref):
    h = pl.program_id(0)
    d = pl.program_id(1)
    t = ATT_T
    i = lax.broadcasted_iota(jnp.int32, (t, t), 0)
    j = lax.broadcasted_iota(jnp.int32, (t, t), 1)
    rel = d * t + i - j
    n = jnp.maximum(rel, 0)
    max_exact = N_BUCKETS // 2
    nf = jnp.maximum(n, 1).astype(F32)
    large = max_exact + (jnp.log(nf / max_exact) / math.log(MAX_DISTANCE / max_exact)
                         * (N_BUCKETS - max_exact)).astype(jnp.int32)
    large = jnp.minimum(large, N_BUCKETS - 1)
    bucket = jnp.where(n < max_exact, n, large)
    val = jnp.zeros((t, t), F32)
    for b in range(N_BUCKETS):
        val = jnp.where(bucket == b, rb_ref[b, h], val)
    val = (val - rb_ref[N_BUCKETS - 1, h]) * LOG2E
    val = jnp.where(rel >= 0, val, NEG_INF)
    o_ref[0:t, :] = val
    o_ref[t:2 * t, :] = val


def _bias_tiles(rel_bias):
    t = ATT_T
    return pl.pallas_call(
        _bias_kernel,
        grid=(N_HEADS, 2),
        in_specs=[pl.BlockSpec(memory_space=pltpu.SMEM)],
        out_specs=pl.BlockSpec((None, None, 2 * t, t), lambda h, d: (h, d, 0, 0)),
        out_shape=jax.ShapeDtypeStruct((N_HEADS, 2, 2 * t, t), F32),
        compiler_params=_params(("arbitrary", "arbitrary")),
        name="bias_tiles",
    )(rel_bias)


def _attn_kernel(q_ref, k_ref, v_ref, bias_ref, lam_ref, sg_ref, o_ref, vx_ref, *, lam_init):
    t = ATT_T
    s_len = k_ref.shape[0]
    vx_ref[:, 0:V_DIM] = v_ref[...]
    vx_ref[:, V_DIM:2 * V_DIM] = jnp.ones((s_len, V_DIM), BF16)
    lam = lam_ref[...]
    lam_full = (jnp.exp(jnp.sum(lam[0:1, :] * lam[1:2, :], axis=-1, keepdims=True))
                - jnp.exp(jnp.sum(lam[2:3, :] * lam[3:4, :], axis=-1, keepdims=True)) + lam_init)
    lane = lax.broadcasted_iota(jnp.int32, (t, V_DIM), 1)
    nt = (((1,), (1,)), ((), ()))
    for qi in reversed(range(s_len // t)):
        q = q_ref[qi * t:(qi + 1) * t, :]
        zero = jnp.zeros_like(q)
        q2 = jnp.concatenate([jnp.where(lane < HEAD_DIM, q, zero),
                              jnp.where(lane >= HEAD_DIM, q, zero)], axis=0)
        parts = []
        if qi >= 2:
            n_far = (qi - 1) * t
            parts.append((0, lax.dot_general(q2, k_ref[0:n_far, :], nt, preferred_element_type=F32)))
        if qi >= 1:
            k0 = (qi - 1) * t
            parts.append((k0, lax.dot_general(q2, k_ref[k0:k0 + t, :], nt,
                                              preferred_element_type=F32) + bias_ref[1]))
        k0 = qi * t
        parts.append((k0, lax.dot_general(q2, k_ref[k0:k0 + t, :], nt,
                                          preferred_element_type=F32) + bias_ref[0]))
        m = None
        for _, sp in parts:
            mp = jnp.max(sp, axis=1, keepdims=True)
            m = mp if m is None else jnp.maximum(m, mp)
        r = None
        for k0, sp in parts:
            p = jnp.exp2(sp - m).astype(BF16)
            rp = _dot(p, vx_ref[k0:k0 + sp.shape[1], :])
            r = rp if r is None else r + rp
        o = r[:, 0:V_DIM] / r[:, V_DIM:2 * V_DIM]
        out = o[0:t, :] - lam_full * o[t:2 * t, :]
        o_ref[qi * t:(qi + 1) * t, :] = (_rms(out, sg_ref[...]) * (1.0 - lam_init)).astype(o_ref.dtype)


def _diff_attention(qkv, bias, lam, subln, lam_init):
    b, s, _ = qkv.shape
    t = ATT_T
    return pl.pallas_call(
        functools.partial(_attn_kernel, lam_init=lam_init),
        grid=(b, N_HEADS),
        in_specs=[
            pl.BlockSpec((None, s, V_DIM), lambda bi, h: (bi, 0, h)),
            pl.BlockSpec((None, s, V_DIM), lambda bi, h: (bi, 0, N_HEADS + h)),
            pl.BlockSpec((None, s, V_DIM), lambda bi, h: (bi, 0, 2 * N_HEADS + h)),
            pl.BlockSpec((None, 2, 2 * t, t), lambda bi, h: (h, 0, 0, 0)),
            pl.BlockSpec((4, HEAD_DIM), lambda bi, h: (0, 0)),
            pl.BlockSpec((1, V_DIM), lambda bi, h: (0, 0)),
        ],
        out_specs=pl.BlockSpec((None, s, V_DIM), lambda bi, h: (bi, 0, h)),
        out_shape=jax.ShapeDtypeStruct((b, s, N_HEADS * V_DIM), BF16),
        scratch_shapes=[pltpu.VMEM((s, 2 * V_DIM), BF16)],
        compiler_params=_params(("parallel", "arbitrary")),
        name="diff_attention",
    )(qkv, qkv, qkv, bias, lam, subln)


CONV_LANES = 128
CONV_ROWS = 64


def _conv_fill(z_ref, halo, sh_ref):
    sh_ref[0, 0:HALO, :] = halo
    sh_ref[0, HALO:HALO + SEQ_TS, :] = z_ref[...]


def _conv_lanes(ls, w_ref, sh_ref, cv_ref):
    ts = SEQ_TS
    ext = HALO + ts
    off = HALO - (CONV_K - 1)
    for r in range(1, SUBLANES):
        sh_ref[r, 0:ext - SUBLANES, ls] = sh_ref[0, r:r + ext - SUBLANES, ls]
    for c in range(ts // CONV_ROWS):
        acc = None
        for k in range(CONV_K):
            a, r = divmod(off + k, SUBLANES)
            row0 = c * CONV_ROWS + SUBLANES * a
            term = w_ref[k:k + 1, ls] * sh_ref[r, row0:row0 + CONV_ROWS, ls]
            acc = term if acc is None else acc + term
        cv_ref[c * CONV_ROWS:(c + 1) * CONV_ROWS, ls] = acc


def _conv_finish(cv_ref, b_ref, lg_ref, lb_ref):
    c = cv_ref[...] + b_ref[...]
    mu = jnp.mean(c, axis=-1, keepdims=True)
    xc = c - mu
    y = xc * lax.rsqrt(jnp.mean(xc * xc, axis=-1, keepdims=True) + LN_EPS) * lg_ref[...] + lb_ref[...]
    return (y * jax.nn.sigmoid(y)).astype(BF16)


def _window_sums(ext_ref, ls, w, lvl_ref):
    ext = ext_ref.shape[0]
    assert w >= 2 and w & (w - 1) == 0 and SUBLANES * (w.bit_length() - 1) <= HALO
    m, level = 1, 0
    while True:
        lo = SUBLANES * (level + 1)
        if level == 0:
            val = ext_ref[lo:ext, ls] + ext_ref[lo - m:ext - m, ls]
        else:
            src = lvl_ref.at[level % 2]
            val = src[lo:ext, :] + src[lo - m:ext - m, :]
        m, level = 2 * m, level + 1
        if m == w:
            return val[HALO - lo:, :]
        lvl_ref[level % 2, lo:ext, :] = val


def _merge_kernel(x_ref, z_ref, zh_ref, a_ref, u_ref, uh_ref, ng_ref, wg_ref,
                  cw_ref, cb_ref, clg_ref, clb_ref, pw_ref, ao_ref,
                  plw_ref, pls_ref, plo_ref, wo_ref, o_ref,
                  sh_ref, cv_ref, ext_ref, lvl_ref, pp_ref):
    ts = SEQ_TS
    i = pl.program_id(1)
    x = x_ref[...]
    h = _rms(x, ng_ref[2:3, :]).astype(BF16)

    _conv_fill(z_ref, jnp.where(i == 0, 0.0, zh_ref[...]), sh_ref)
    for c in range(0, D_MODEL, CONV_LANES):
        _conv_lanes(slice(c, c + CONV_LANES), cw_ref, sh_ref, cv_ref)
    y_conv = _dot_cols(_conv_finish(cv_ref, cb_ref, clg_ref, clb_ref), pw_ref)
    y_attn = _dot_cols(a_ref[...], ao_ref)
    merged = jax.nn.sigmoid(_dot_cols(h, wg_ref, col0=0)) * y_conv
    merged = merged + jax.nn.sigmoid(_dot_cols(h, wg_ref, col0=D_MODEL)) * y_attn

    ext_ref[0:HALO, :] = jnp.where(i == 0, 0.0, uh_ref[...])
    ext_ref[HALO:HALO + ts, :] = u_ref[...]
    pos = i * ts + lax.broadcasted_iota(jnp.int32, (ts, 1), 0)
    for g, w in enumerate(POOL_WINDOWS):
        ls = slice(g * POOL_GD, (g + 1) * POOL_GD)
        tot = _window_sums(ext_ref, ls, w, lvl_ref)
        cnt = jnp.minimum(pos + 1, w).astype(F32)
        pooled = (tot / cnt - ext_ref[HALO:HALO + ts, ls]).astype(BF16)
        yg = _dot(pooled, plw_ref[g]) * pls_ref[:, ls]
        pp_ref[:, ls] = yg.astype(BF16)

    y_pool = _dot_cols(pp_ref[...], plo_ref)
    merged = merged + jax.nn.sigmoid(_dot_cols(h, wg_ref, col0=2 * D_MODEL)) * y_pool
    y = _dot_cols(merged.astype(BF16), wo_ref)
    o_ref[...] = x + _rms(y, ng_ref[3:4, :])


def _merge(x, z, attn, u_pool, ng, w_in, conv_dw, conv_b, conv_lg, conv_lb, conv_pw, attn_o,
           pool_w, pool_scale, pool_o, w_out, l):
    b, s, _ = x.shape
    ts = SEQ_TS
    tile = pl.BlockSpec((None, ts, D_MODEL), lambda bi, i: (bi, i, 0))
    halo = pl.BlockSpec((None, HALO, D_MODEL),
                        lambda bi, i: (bi, jnp.maximum(i * (ts // HALO) - 1, 0), 0))
    sq = _resident((None, D_MODEL, D_MODEL), lambda bi, i: (l, 0, 0))
    vec = _resident((1, D_MODEL), lambda bi, i: (0, 0))
    return pl.pallas_call(
        _merge_kernel,
        grid=(b, s // ts),
        in_specs=[
            tile, tile, halo, tile, tile, halo,
            _resident((None, 8, D_MODEL), lambda bi, i: (l, 0, 0)),
            _resident((None, D_MODEL, 3 * D_MODEL), lambda bi, i: (l, 0, COL_GATES // (3 * D_MODEL))),
            _resident((CONV_K, D_MODEL), lambda bi, i: (0, 0)),
            vec, vec, vec,
            sq, sq,
            _resident((None, len(POOL_WINDOWS), POOL_GD, POOL_GD), lambda bi, i: (l, 0, 0, 0)),
            vec,
            sq, sq,
        ],
        out_specs=tile,
        out_shape=jax.ShapeDtypeStruct((b, s, D_MODEL), F32),
        scratch_shapes=[
            pltpu.VMEM((SUBLANES, HALO + ts, D_MODEL), F32),
            pltpu.VMEM((ts, D_MODEL), F32),
            pltpu.VMEM((HALO + ts, D_MODEL), F32),
            pltpu.VMEM((2, HALO + ts, POOL_GD), F32),
            pltpu.VMEM((ts, D_MODEL), BF16),
        ],
        compiler_params=_params(("parallel", "arbitrary")),
        name="merge",
    )(x, z, z, attn, u_pool, u_pool, ng, w_in, conv_dw, conv_b, conv_lg, conv_lb, conv_pw, attn_o,
      pool_w, pool_scale, pool_o, w_out)


def _xkv_kernel(m_ref, g_ref, w_ref, k_ref, v_ref):
    mn = _rms(m_ref[...], g_ref[...]).astype(BF16)
    k_ref[...] = _dot_cols(mn, w_ref, col0=0).astype(k_ref.dtype)
    v_ref[...] = _dot_cols(mn, w_ref, col0=D_MODEL).astype(v_ref.dtype)


def _xattn_kv(mem, g, wkv, l):
    b, m, _ = mem.shape
    out = pl.BlockSpec((None, m, D_MODEL), lambda bi: (bi, 0, 0))
    return pl.pallas_call(
        _xkv_kernel,
        grid=(b,),
        in_specs=[
            pl.BlockSpec((None, m, D_MODEL), lambda bi: (bi, 0, 0)),
            pl.BlockSpec((1, D_MODEL), lambda bi: (0, 0)),
            _resident((None, D_MODEL, 2 * D_MODEL), lambda bi: (l, 0, 0)),
        ],
        out_specs=[out, out],
        out_shape=[jax.ShapeDtypeStruct((b, m, D_MODEL), BF16)] * 2,
        compiler_params=_params(("parallel",)),
        name="xattn_kv",
    )(mem, g, wkv)


def _xattn_kernel(x_ref, k_ref, v_ref, ng_ref, wq_ref, wo_ref, o_ref, oh_ref):
    x = x_ref[...]
    h = _rms(x, ng_ref[4:5, :]).astype(BF16)
    q = (_dot_cols(h, wq_ref) * (X_HEAD_DIM ** -0.5)).astype(BF16)
    for hd in range(X_HEADS):
        ls = slice(hd * X_HEAD_DIM, (hd + 1) * X_HEAD_DIM)
        s = lax.dot_general(q[:, ls], k_ref[:, ls], (((1,), (1,)), ((), ())),
                            preferred_element_type=F32)
        p = jnp.exp(s - jnp.max(s, axis=-1, keepdims=True))
        denom = jnp.sum(p, axis=-1, keepdims=True)
        oh_ref[:, ls] = (_dot(p.astype(BF16), v_ref[:, ls]) / denom).astype(BF16)
    y = _dot_cols(oh_ref[...], wo_ref)
    o_ref[...] = x + _rms(y, ng_ref[5:6, :])


def _xattn(x, k, v, ng, wq, wo, l):
    b, s, _ = x.shape
    m = k.shape[1]
    ts = XATT_TS
    tile = pl.BlockSpec((None, ts, D_MODEL), lambda bi, i: (bi, i, 0))
    kv = pl.BlockSpec((None, m, D_MODEL), lambda bi, i: (bi, 0, 0))
    sq = _resident((None, D_MODEL, D_MODEL), lambda bi, i: (l, 0, 0))
    return pl.pallas_call(
        _xattn_kernel,
        grid=(b, s // ts),
        in_specs=[tile, kv, kv, _resident((None, 8, D_MODEL), lambda bi, i: (l, 0, 0)), sq, sq],
        out_specs=tile,
        out_shape=jax.ShapeDtypeStruct((b, s, D_MODEL), F32),
        scratch_shapes=[pltpu.VMEM((ts, D_MODEL), BF16)],
        compiler_params=_params(("parallel", "arbitrary")),
        name="xattn",
    )(x, k, v, ng, wq, wo)


def kernel(x, mem, norm_g, ffn_w1, ffn_w3, ffn_w2, w_in, conv_dw, conv_dw_b, conv_ln_g, conv_ln_b, conv_pw, attn_lam, attn_subln, attn_o, rel_bias, pool_w, pool_scale, pool_o, w_out, mem_norm, xattn_q, xattn_kv, xattn_o):
    b, s, d = x.shape
    n = b * s
    w1, w3, w2 = ffn_w1.astype(BF16), ffn_w3.astype(BF16), ffn_w2.astype(BF16)
    w_in_b = w_in.astype(BF16)
    conv_pw_b, attn_o_b, pool_w_b = conv_pw.astype(BF16), attn_o.astype(BF16), pool_w.astype(BF16)
    pool_o_b, w_out_b = pool_o.astype(BF16), w_out.astype(BF16)
    xq_b, xkv_b, xo_b = xattn_q.astype(BF16), xattn_kv.astype(BF16), xattn_o.astype(BF16)
    bias = _bias_tiles(rel_bias)

    for l in range(DEPTH):
        lam_init = 0.8 - 0.6 * math.exp(-0.3 * l)
        x2 = _ffn(x.reshape(n, d), norm_g, w1, w3, w2, l, 0)
        z, qkv, u_pool = _in_proj(x2, norm_g, w_in_b, l)
        attn = _diff_attention(qkv.reshape(b, s, 3 * d), bias, attn_lam[l], attn_subln[l][None, :],
                               lam_init)
        x = _merge(x2.reshape(b, s, d), z.reshape(b, s, d), attn, u_pool.reshape(b, s, d), norm_g,
                   w_in_b, conv_dw[l], conv_dw_b[l][None, :], conv_ln_g[l][None, :],
                   conv_ln_b[l][None, :], conv_pw_b, attn_o_b, pool_w_b, pool_scale[l][None, :],
                   pool_o_b, w_out_b, l)
        xk, xv = _xattn_kv(mem, mem_norm[l][None, :], xkv_b, l)
        x = _xattn(x, xk, xv, norm_g, xq_b, xo_b, l)
        x = _ffn(x.reshape(n, d), norm_g, w1, w3, w2, l, 1).reshape(b, s, d)
    return x
```

```python
import functools
import math

import jax
import jax.numpy as jnp
from jax import lax
from jax.experimental import pallas as pl
from jax.experimental.pallas import tpu as pltpu

D_MODEL = 1024
DEPTH = 4
D_FF = 2816
CONV_K = 31
N_HEADS = 8
HEAD_DIM = 64
V_DIM = 2 * HEAD_DIM
N_BUCKETS = 32
MAX_DISTANCE = 128
POOL_WINDOWS = (2, 4, 8, 16)
POOL_GD = D_MODEL // len(POOL_WINDOWS)
X_HEADS = 4
X_HEAD_DIM = D_MODEL // X_HEADS
COL_CONV_A = 0
COL_CONV_G = D_MODEL
COL_QKV = 2 * D_MODEL
COL_POOL = 5 * D_MODEL
COL_GATES = 6 * D_MODEL
RMS_EPS = 1e-6
LN_EPS = 1e-5
NEG_INF = -1e30
LOG2E = math.log2(math.e)

F32 = jnp.float32
BF16 = jnp.bfloat16

MXU_N = 256
SUBLANES = 8
HALO = 32
FFN_TM = 512
FFN_TF = 256
PROJ_TM = 512
SEQ_TS = 256
XATT_TS = 512
ATT_T = 256
VMEM_LIMIT = 56 * 1024 * 1024


def _rms(x, g):
    ms = jnp.mean(x * x, axis=-1, keepdims=True)
    return x * lax.rsqrt(ms + RMS_EPS) * g


def _dot(a, b):
    return jnp.dot(a, b, preferred_element_type=F32)


def _dot_cols(a, w_ref, rows=slice(None), col0=0, ncols=D_MODEL):
    return jnp.concatenate(
        [_dot(a, w_ref[rows, c:c + MXU_N]) for c in range(col0, col0 + ncols, MXU_N)], axis=1)


def _params(sem):
    return pltpu.CompilerParams(dimension_semantics=sem, vmem_limit_bytes=VMEM_LIMIT)


def _resident(shape, index_map):
    return pl.BlockSpec(shape, index_map, pipeline_mode=pl.Buffered(1))


def _ffn_kernel(x_ref, ng_ref, w1_ref, w3_ref, w2_ref, o_ref, *, g_in, g_out):
    x = x_ref[...]
    h = _rms(x, ng_ref[g_in:g_in + 1, :]).astype(BF16)
    acc = None
    for c in range(D_FF // FFN_TF):
        sl = slice(c * FFN_TF, (c + 1) * FFN_TF)
        a = _dot(h, w1_ref[:, sl])
        b = _dot(h, w3_ref[:, sl])
        t = (a * jax.nn.sigmoid(a) * b).astype(BF16)
        y = _dot_cols(t, w2_ref, rows=sl)
        acc = y if acc is None else acc + y
    o_ref[...] = x + 0.5 * _rms(acc, ng_ref[g_out:g_out + 1, :])


def _ffn(x2, ng, w1, w3, w2, l, s):
    n = x2.shape[0]
    g_in, g_out = (0, 1) if s == 0 else (6, 7)
    return pl.pallas_call(
        functools.partial(_ffn_kernel, g_in=g_in, g_out=g_out),
        grid=(n // FFN_TM,),
        in_specs=[
            pl.BlockSpec((FFN_TM, D_MODEL), lambda i: (i, 0)),
            _resident((None, 8, D_MODEL), lambda i: (l, 0, 0)),
            _resident((None, None, D_MODEL, D_FF), lambda i: (l, s, 0, 0)),
            _resident((None, None, D_MODEL, D_FF), lambda i: (l, s, 0, 0)),
            _resident((None, None, D_FF, D_MODEL), lambda i: (l, s, 0, 0)),
        ],
        out_specs=pl.BlockSpec((FFN_TM, D_MODEL), lambda i: (i, 0)),
        out_shape=jax.ShapeDtypeStruct((n, D_MODEL), F32),
        compiler_params=_params(("parallel",)),
        name="ffn",
    )(x2, ng, w1, w3, w2)


def _in_proj_kernel(x_ref, ng_ref, w_ref, z_ref, qkv_ref, u_ref):
    d = D_MODEL
    h = _rms(x_ref[...], ng_ref[2:3, :]).astype(BF16)
    a = _dot_cols(h, w_ref, col0=COL_CONV_A)
    gt = _dot_cols(h, w_ref, col0=COL_CONV_G)
    z_ref[...] = a * jax.nn.sigmoid(gt)
    q = _dot_cols(h, w_ref, col0=COL_QKV) * (HEAD_DIM ** -0.5 * LOG2E)
    qkv_ref[:, 0:d] = q.astype(BF16)
    qkv_ref[:, d:3 * d] = _dot_cols(h, w_ref, col0=COL_QKV + d, ncols=2 * d).astype(BF16)
    u_ref[...] = _dot_cols(h, w_ref, col0=COL_POOL)


def _in_proj(x2, ng, w_in, l):
    n = x2.shape[0]
    row = pl.BlockSpec((PROJ_TM, D_MODEL), lambda i: (i, 0))
    return pl.pallas_call(
        _in_proj_kernel,
        grid=(n // PROJ_TM,),
        in_specs=[
            row,
            _resident((None, 8, D_MODEL), lambda i: (l, 0, 0)),
            _resident((None, D_MODEL, COL_GATES), lambda i: (l, 0, 0)),
        ],
        out_specs=[row, pl.BlockSpec((PROJ_TM, 3 * D_MODEL), lambda i: (i, 0)), row],
        out_shape=[jax.ShapeDtypeStruct((n, D_MODEL), F32),
                   jax.ShapeDtypeStruct((n, 3 * D_MODEL), BF16),
                   jax.ShapeDtypeStruct((n, D_MODEL), F32)],
        compiler_params=_params(("parallel",)),
        name="in_proj",
    )(x2, ng, w_in)


def _bias_kernel(rb_ref, o_ref):
    h = pl.program_id(0)
    d = pl.program_id(1)
    t = ATT_T
    i = lax.broadcasted_iota(jnp.int32, (t, t), 0)
    j = lax.broadcasted_iota(jnp.int32, (t, t), 1)
    rel = d * t + i - j
    n = jnp.maximum(rel, 0)
    max_exact = N_BUCKETS // 2
    nf = jnp.maximum(n, 1).astype(F32)
    large = max_exact + (jnp.log(nf / max_exact) / math.log(MAX_DISTANCE / max_exact)
                         * (N_BUCKETS - max_exact)).astype(jnp.int32)
    large = jnp.minimum(large, N_BUCKETS - 1)
    bucket = jnp.where(n < max_exact, n, large)
    val = jnp.zeros((t, t), F32)
    for b in range(N_BUCKETS):
        val = jnp.where(bucket == b, rb_ref[b, h], val)
    val = (val - rb_ref[N_BUCKETS - 1, h]) * LOG2E
    val = jnp.where(rel >= 0, val, NEG_INF)
    o_ref[0:t, :] = val
    o_ref[t:2 * t, :] = val


def _bias_tiles(rel_bias):
    t = ATT_T
    return pl.pallas_call(
        _bias_kernel,
        grid=(N_HEADS, 2),
        in_specs=[pl.BlockSpec(memory_space=pltpu.SMEM)],
        out_specs=pl.BlockSpec((None, None, 2 * t, t), lambda h, d: (h, d, 0, 0)),
        out_shape=jax.ShapeDtypeStruct((N_HEADS, 2, 2 * t, t), F32),
        compiler_params=_params(("arbitrary", "arbitrary")),
        name="bias_tiles",
    )(rel_bias)


def _attn_kernel(q_ref, k_ref, v_ref, bias_ref, lam_ref, sg_ref, o_ref, vx_ref, *, lam_init):
    t = ATT_T
    s_len = k_ref.shape[0]
    vx_ref[:, 0:V_DIM] = v_ref[...]
    vx_ref[:, V_DIM:2 * V_DIM] = jnp.ones((s_len, V_DIM), BF16)
    lam = lam_ref[...]
    lam_full = (jnp.exp(jnp.sum(lam[0:1, :] * lam[1:2, :], axis=-1, keepdims=True))
                - jnp.exp(jnp.sum(lam[2:3, :] * lam[3:4, :], axis=-1, keepdims=True)) + lam_init)
    lane = lax.broadcasted_iota(jnp.int32, (t, V_DIM), 1)
    nt = (((1,), (1,)), ((), ()))

    def logits(qi):
        q = q_ref[qi * t:(qi + 1) * t, :]
        zero = jnp.zeros_like(q)
        q2 = jnp.concatenate([jnp.where(lane < HEAD_DIM, q, zero),
                              jnp.where(lane >= HEAD_DIM, q, zero)], axis=0)
        ranges = []
        if qi >= 2:
            ranges.append((0, (qi - 1) * t, None))
        if qi >= 1:
            ranges.append(((qi - 1) * t, t, 1))
        ranges.append((qi * t, t, 0))
        parts = []
        m = None
        for k0, n, bias_idx in ranges:
            sp = lax.dot_general(q2, k_ref[k0:k0 + n, :], nt, preferred_element_type=F32)
            if bias_idx is not None:
                sp = sp + bias_ref[bias_idx]
            parts.append((k0, sp))
            mp = jnp.max(sp, axis=1, keepdims=True)
            m = mp if m is None else jnp.maximum(m, mp)
        return parts, m

    order = list(reversed(range(s_len // t)))
    pending = logits(order[0])
    for idx, qi in enumerate(order):
        parts, m = pending
        if idx + 1 < len(order):
            pending = logits(order[idx + 1])
        r = None
        for k0, sp in parts:
            p = jnp.exp2((sp - m).astype(BF16))
            rp = _dot(p, vx_ref[k0:k0 + sp.shape[1], :])
            r = rp if r is None else r + rp
        o = r[:, 0:V_DIM] / r[:, V_DIM:2 * V_DIM]
        out = o[0:t, :] - lam_full * o[t:2 * t, :]
        o_ref[qi * t:(qi + 1) * t, :] = (_rms(out, sg_ref[...]) * (1.0 - lam_init)).astype(o_ref.dtype)


def _diff_attention(qkv, bias, lam, subln, lam_init):
    b, s, _ = qkv.shape
    t = ATT_T
    return pl.pallas_call(
        functools.partial(_attn_kernel, lam_init=lam_init),
        grid=(b, N_HEADS),
        in_specs=[
            pl.BlockSpec((None, s, V_DIM), lambda bi, h: (bi, 0, h)),
            pl.BlockSpec((None, s, V_DIM), lambda bi, h: (bi, 0, N_HEADS + h)),
            pl.BlockSpec((None, s, V_DIM), lambda bi, h: (bi, 0, 2 * N_HEADS + h)),
            pl.BlockSpec((None, 2, 2 * t, t), lambda bi, h: (h, 0, 0, 0)),
            pl.BlockSpec((4, HEAD_DIM), lambda bi, h: (0, 0)),
            pl.BlockSpec((1, V_DIM), lambda bi, h: (0, 0)),
        ],
        out_specs=pl.BlockSpec((None, s, V_DIM), lambda bi, h: (bi, 0, h)),
        out_shape=jax.ShapeDtypeStruct((b, s, N_HEADS * V_DIM), BF16),
        scratch_shapes=[pltpu.VMEM((s, 2 * V_DIM), BF16)],
        compiler_params=_params(("parallel", "arbitrary")),
        name="diff_attention",
    )(qkv, qkv, qkv, bias, lam, subln)


CONV_LANES = 128
CONV_ROWS = 64


def _conv_fill(z_ref, halo, sh_ref):
    sh_ref[0, 0:HALO, :] = halo
    sh_ref[0, HALO:HALO + SEQ_TS, :] = z_ref[...]


def _conv_lanes(ls, w_ref, sh_ref, cv_ref):
    ts = SEQ_TS
    ext = HALO + ts
    off = HALO - (CONV_K - 1)
    for r in range(1, SUBLANES):
        sh_ref[r, 0:ext - SUBLANES, ls] = sh_ref[0, r:r + ext - SUBLANES, ls]
    for c in range(ts // CONV_ROWS):
        acc = None
        for k in range(CONV_K):
            a, r = divmod(off + k, SUBLANES)
            row0 = c * CONV_ROWS + SUBLANES * a
            term = w_ref[k:k + 1, ls] * sh_ref[r, row0:row0 + CONV_ROWS, ls]
            acc = term if acc is None else acc + term
        cv_ref[c * CONV_ROWS:(c + 1) * CONV_ROWS, ls] = acc


def _conv_finish(cv_ref, b_ref, lg_ref, lb_ref):
    c = cv_ref[...] + b_ref[...]
    mu = jnp.mean(c, axis=-1, keepdims=True)
    xc = c - mu
    y = xc * lax.rsqrt(jnp.mean(xc * xc, axis=-1, keepdims=True) + LN_EPS) * lg_ref[...] + lb_ref[...]
    return (y * jax.nn.sigmoid(y)).astype(BF16)


def _window_sums(ext_ref, ls, w, lvl_ref):
    ext = ext_ref.shape[0]
    assert w >= 2 and w & (w - 1) == 0 and SUBLANES * (w.bit_length() - 1) <= HALO
    m, level = 1, 0
    while True:
        lo = SUBLANES * (level + 1)
        if level == 0:
            val = ext_ref[lo:ext, ls] + ext_ref[lo - m:ext - m, ls]
        else:
            src = lvl_ref.at[level % 2]
            val = src[lo:ext, :] + src[lo - m:ext - m, :]
        m, level = 2 * m, level + 1
        if m == w:
            return val[HALO - lo:, :]
        lvl_ref[level % 2, lo:ext, :] = val


def _merge_kernel(x_ref, z_ref, zh_ref, a_ref, u_ref, uh_ref, ng_ref, wg_ref,
                  cw_ref, cb_ref, clg_ref, clb_ref, pw_ref, ao_ref,
                  plw_ref, pls_ref, plo_ref, wo_ref, o_ref,
                  sh_ref, cv_ref, ext_ref, lvl_ref, pp_ref):
    ts = SEQ_TS
    i = pl.program_id(1)
    x = x_ref[...]
    h = _rms(x, ng_ref[2:3, :]).astype(BF16)

    _conv_fill(z_ref, jnp.where(i == 0, 0.0, zh_ref[...]), sh_ref)
    for c in range(0, D_MODEL, CONV_LANES):
        _conv_lanes(slice(c, c + CONV_LANES), cw_ref, sh_ref, cv_ref)
    y_conv = _dot_cols(_conv_finish(cv_ref, cb_ref, clg_ref, clb_ref), pw_ref)
    y_attn = _dot_cols(a_ref[...], ao_ref)
    merged = jax.nn.sigmoid(_dot_cols(h, wg_ref, col0=0)) * y_conv
    merged = merged + jax.nn.sigmoid(_dot_cols(h, wg_ref, col0=D_MODEL)) * y_attn

    ext_ref[0:HALO, :] = jnp.where(i == 0, 0.0, uh_ref[...])
    ext_ref[HALO:HALO + ts, :] = u_ref[...]
    pos = i * ts + lax.broadcasted_iota(jnp.int32, (ts, 1), 0)
    for g, w in enumerate(POOL_WINDOWS):
        ls = slice(g * POOL_GD, (g + 1) * POOL_GD)
        tot = _window_sums(ext_ref, ls, w, lvl_ref)
        cnt = jnp.minimum(pos + 1, w).astype(F32)
        pooled = (tot / cnt - ext_ref[HALO:HALO + ts, ls]).astype(BF16)
        yg = _dot(pooled, plw_ref[g]) * pls_ref[:, ls]
        pp_ref[:, ls] = yg.astype(BF16)

    y_pool = _dot_cols(pp_ref[...], plo_ref)
    merged = merged + jax.nn.sigmoid(_dot_cols(h, wg_ref, col0=2 * D_MODEL)) * y_pool
    y = _dot_cols(merged.astype(BF16), wo_ref)
    o_ref[...] = x + _rms(y, ng_ref[3:4, :])


def _merge(x, z, attn, u_pool, ng, w_in, conv_dw, conv_b, conv_lg, conv_lb, conv_pw, attn_o,
           pool_w, pool_scale, pool_o, w_out, l):
    b, s, _ = x.shape
    ts = SEQ_TS
    tile = pl.BlockSpec((None, ts, D_MODEL), lambda bi, i: (bi, i, 0))
    halo = pl.BlockSpec((None, HALO, D_MODEL),
                        lambda bi, i: (bi, jnp.maximum(i * (ts // HALO) - 1, 0), 0))
    sq = _resident((None, D_MODEL, D_MODEL), lambda bi, i: (l, 0, 0))
    vec = _resident((1, D_MODEL), lambda bi, i: (0, 0))
    return pl.pallas_call(
        _merge_kernel,
        grid=(b, s // ts),
        in_specs=[
            tile, tile, halo, tile, tile, halo,
            _resident((None, 8, D_MODEL), lambda bi, i: (l, 0, 0)),
            _resident((None, D_MODEL, 3 * D_MODEL), lambda bi, i: (l, 0, COL_GATES // (3 * D_MODEL))),
            _resident((CONV_K, D_MODEL), lambda bi, i: (0, 0)),
            vec, vec, vec,
            sq, sq,
            _resident((None, len(POOL_WINDOWS), POOL_GD, POOL_GD), lambda bi, i: (l, 0, 0, 0)),
            vec,
            sq, sq,
        ],
        out_specs=tile,
        out_shape=jax.ShapeDtypeStruct((b, s, D_MODEL), F32),
        scratch_shapes=[
            pltpu.VMEM((SUBLANES, HALO + ts, D_MODEL), F32),
            pltpu.VMEM((ts, D_MODEL), F32),
            pltpu.VMEM((HALO + ts, D_MODEL), F32),
            pltpu.VMEM((2, HALO + ts, POOL_GD), F32),
            pltpu.VMEM((ts, D_MODEL), BF16),
        ],
        compiler_params=_params(("parallel", "arbitrary")),
        name="merge",
    )(x, z, z, attn, u_pool, u_pool, ng, w_in, conv_dw, conv_b, conv_lg, conv_lb, conv_pw, attn_o,
      pool_w, pool_scale, pool_o, w_out)


def _xkv_kernel(m_ref, g_ref, w_ref, k_ref, v_ref):
    mn = _rms(m_ref[...], g_ref[...]).astype(BF16)
    k_ref[...] = _dot_cols(mn, w_ref, col0=0).astype(k_ref.dtype)
    v_ref[...] = _dot_cols(mn, w_ref, col0=D_MODEL).astype(v_ref.dtype)


def _xattn_kv(mem, g, wkv, l):
    b, m, _ = mem.shape
    out = pl.BlockSpec((None, m, D_MODEL), lambda bi: (bi, 0, 0))
    return pl.pallas_call(
        _xkv_kernel,
        grid=(b,),
        in_specs=[
            pl.BlockSpec((None, m, D_MODEL), lambda bi: (bi, 0, 0)),
            pl.BlockSpec((1, D_MODEL), lambda bi: (0, 0)),
            _resident((None, D_MODEL, 2 * D_MODEL), lambda bi: (l, 0, 0)),
        ],
        out_specs=[out, out],
        out_shape=[jax.ShapeDtypeStruct((b, m, D_MODEL), BF16)] * 2,
        compiler_params=_params(("parallel",)),
        name="xattn_kv",
    )(mem, g, wkv)


def _xattn_kernel(x_ref, k_ref, v_ref, ng_ref, wq_ref, wo_ref, o_ref, oh_ref):
    x = x_ref[...]
    h = _rms(x, ng_ref[4:5, :]).astype(BF16)
    q = (_dot_cols(h, wq_ref) * (X_HEAD_DIM ** -0.5)).astype(BF16)
    for hd in range(X_HEADS):
        ls = slice(hd * X_HEAD_DIM, (hd + 1) * X_HEAD_DIM)
        s = lax.dot_general(q[:, ls], k_ref[:, ls], (((1,), (1,)), ((), ())),
                            preferred_element_type=F32)
        p = jnp.exp(s - jnp.max(s, axis=-1, keepdims=True))
        denom = jnp.sum(p, axis=-1, keepdims=True)
        oh_ref[:, ls] = (_dot(p.astype(BF16), v_ref[:, ls]) / denom).astype(BF16)
    y = _dot_cols(oh_ref[...], wo_ref)
    o_ref[...] = x + _rms(y, ng_ref[5:6, :])


def _xattn(x, k, v, ng, wq, wo, l):
    b, s, _ = x.shape
    m = k.shape[1]
    ts = XATT_TS
    tile = pl.BlockSpec((None, ts, D_MODEL), lambda bi, i: (bi, i, 0))
    kv = pl.BlockSpec((None, m, D_MODEL), lambda bi, i: (bi, 0, 0))
    sq = _resident((None, D_MODEL, D_MODEL), lambda bi, i: (l, 0, 0))
    return pl.pallas_call(
        _xattn_kernel,
        grid=(b, s // ts),
        in_specs=[tile, kv, kv, _resident((None, 8, D_MODEL), lambda bi, i: (l, 0, 0)), sq, sq],
        out_specs=tile,
        out_shape=jax.ShapeDtypeStruct((b, s, D_MODEL), F32),
        scratch_shapes=[pltpu.VMEM((ts, D_MODEL), BF16)],
        compiler_params=_params(("parallel", "arbitrary")),
        name="xattn",
    )(x, k, v, ng, wq, wo)


def kernel(x, mem, norm_g, ffn_w1, ffn_w3, ffn_w2, w_in, conv_dw, conv_dw_b, conv_ln_g, conv_ln_b, conv_pw, attn_lam, attn_subln, attn_o, rel_bias, pool_w, pool_scale, pool_o, w_out, mem_norm, xattn_q, xattn_kv, xattn_o):
    b, s, d = x.shape
    n = b * s
    w1, w3, w2 = ffn_w1.astype(BF16), ffn_w3.astype(BF16), ffn_w2.astype(BF16)
    w_in_b = w_in.astype(BF16)
    conv_pw_b, attn_o_b, pool_w_b = conv_pw.astype(BF16), attn_o.astype(BF16), pool_w.astype(BF16)
    pool_o_b, w_out_b = pool_o.astype(BF16), w_out.astype(BF16)
    xq_b, xkv_b, xo_b = xattn_q.astype(BF16), xattn_kv.astype(BF16), xattn_o.astype(BF16)
    bias = _bias_tiles(rel_bias)

    for l in range(DEPTH):
        lam_init = 0.8 - 0.6 * math.exp(-0.3 * l)
        x2 = _ffn(x.reshape(n, d), norm_g, w1, w3, w2, l, 0)
        z, qkv, u_pool = _in_proj(x2, norm_g, w_in_b, l)
        attn = _diff_attention(qkv.reshape(b, s, 3 * d), bias, attn_lam[l], attn_subln[l][None, :],
                               lam_init)
        x = _merge(x2.reshape(b, s, d), z.reshape(b, s, d), attn, u_pool.reshape(b, s, d), norm_g,
                   w_in_b, conv_dw[l], conv_dw_b[l][None, :], conv_ln_g[l][None, :],
                   conv_ln_b[l][None, :], conv_pw_b, attn_o_b, pool_w_b, pool_scale[l][None, :],
                   pool_o_b, w_out_b, l)
        xk, xv = _xattn_kv(mem, mem_norm[l][None, :], xkv_b, l)
        x = _xattn(x, xk, xv, norm_g, xq_b, xo_b, l)
        x = _ffn(x.reshape(n, d), norm_g, w1, w3, w2, l, 1).reshape(b, s, d)
    return x
```

```python
import functools
import math

import jax
import jax.numpy as jnp
from jax import lax
from jax.experimental import pallas as pl
from jax.experimental.pallas import tpu as pltpu

D_MODEL = 1024
DEPTH = 4
D_FF = 2816
CONV_K = 31
N_HEADS = 8
HEAD_DIM = 64
V_DIM = 2 * HEAD_DIM
N_BUCKETS = 32
MAX_DISTANCE = 128
POOL_WINDOWS = (2, 4, 8, 16)
POOL_GD = D_MODEL // len(POOL_WINDOWS)
X_HEADS = 4
X_HEAD_DIM = D_MODEL // X_HEADS
COL_CONV_A = 0
COL_CONV_G = D_MODEL
COL_QKV = 2 * D_MODEL
COL_POOL = 5 * D_MODEL
COL_GATES = 6 * D_MODEL
RMS_EPS = 1e-6
LN_EPS = 1e-5
NEG_INF = -1e30
LOG2E = math.log2(math.e)

F32 = jnp.float32
BF16 = jnp.bfloat16

MXU_N = 256
SUBLANES = 8
HALO = 32
FFN_TM = 512
FFN_TF = 256
PROJ_TM = 512
SEQ_TS = 256
XATT_TS = 512
ATT_T = 256
VMEM_LIMIT = 56 * 1024 * 1024


def _rms(x, g):
    ms = jnp.mean(x * x, axis=-1, keepdims=True)
    return x * lax.rsqrt(ms + RMS_EPS) * g


def _dot(a, b):
    return jnp.dot(a, b, preferred_element_type=F32)


def _dot_cols(a, w_ref, rows=slice(None), col0=0, ncols=D_MODEL):
    return jnp.concatenate(
        [_dot(a, w_ref[rows, c:c + MXU_N]) for c in range(col0, col0 + ncols, MXU_N)], axis=1)


def _params(sem):
    return pltpu.CompilerParams(dimension_semantics=sem, vmem_limit_bytes=VMEM_LIMIT)


def _resident(shape, index_map):
    return pl.BlockSpec(shape, index_map, pipeline_mode=pl.Buffered(1))


def _ffn_kernel(x_ref, ng_ref, w1_ref, w3_ref, w2_ref, o_ref, *, g_in, g_out):
    x = x_ref[...]
    h = _rms(x, ng_ref[g_in:g_in + 1, :]).astype(BF16)
    acc = None
    for c in range(D_FF // FFN_TF):
        sl = slice(c * FFN_TF, (c + 1) * FFN_TF)
        a = _dot(h, w1_ref[:, sl])
        b = _dot(h, w3_ref[:, sl])
        t = (a * jax.nn.sigmoid(a) * b).astype(BF16)
        y = _dot_cols(t, w2_ref, rows=sl)
        acc = y if acc is None else acc + y
    o_ref[...] = x + 0.5 * _rms(acc, ng_ref[g_out:g_out + 1, :])


def _ffn(x2, ng, w1, w3, w2, l, s):
    n = x2.shape[0]
    g_in, g_out = (0, 1) if s == 0 else (6, 7)
    return pl.pallas_call(
        functools.partial(_ffn_kernel, g_in=g_in, g_out=g_out),
        grid=(n // FFN_TM,),
        in_specs=[
            pl.BlockSpec((FFN_TM, D_MODEL), lambda i: (i, 0)),
            _resident((None, 8, D_MODEL), lambda i: (l, 0, 0)),
            _resident((None, None, D_MODEL, D_FF), lambda i: (l, s, 0, 0)),
            _resident((None, None, D_MODEL, D_FF), lambda i: (l, s, 0, 0)),
            _resident((None, None, D_FF, D_MODEL), lambda i: (l, s, 0, 0)),
        ],
        out_specs=pl.BlockSpec((FFN_TM, D_MODEL), lambda i: (i, 0)),
        out_shape=jax.ShapeDtypeStruct((n, D_MODEL), F32),
        compiler_params=_params(("parallel",)),
        name="ffn",
    )(x2, ng, w1, w3, w2)


def _in_proj_kernel(x_ref, ng_ref, w_ref, z_ref, qkv_ref, u_ref):
    d = D_MODEL
    h = _rms(x_ref[...], ng_ref[2:3, :]).astype(BF16)
    a = _dot_cols(h, w_ref, col0=COL_CONV_A)
    gt = _dot_cols(h, w_ref, col0=COL_CONV_G)
    z_ref[...] = a * jax.nn.sigmoid(gt)
    q = _dot_cols(h, w_ref, col0=COL_QKV) * (HEAD_DIM ** -0.5 * LOG2E)
    qkv_ref[:, 0:d] = q.astype(BF16)
    qkv_ref[:, d:3 * d] = _dot_cols(h, w_ref, col0=COL_QKV + d, ncols=2 * d).astype(BF16)
    u_ref[...] = _dot_cols(h, w_ref, col0=COL_POOL)


def _in_proj(x2, ng, w_in, l):
    n = x2.shape[0]
    row = pl.BlockSpec((PROJ_TM, D_MODEL), lambda i: (i, 0))
    return pl.pallas_call(
        _in_proj_kernel,
        grid=(n // PROJ_TM,),
        in_specs=[
            row,
            _resident((None, 8, D_MODEL), lambda i: (l, 0, 0)),
            _resident((None, D_MODEL, COL_GATES), lambda i: (l, 0, 0)),
        ],
        out_specs=[row, pl.BlockSpec((PROJ_TM, 3 * D_MODEL), lambda i: (i, 0)), row],
        out_shape=[jax.ShapeDtypeStruct((n, D_MODEL), F32),
                   jax.ShapeDtypeStruct((n, 3 * D_MODEL), BF16),
                   jax.ShapeDtypeStruct((n, D_MODEL), F32)],
        compiler_params=_params(("parallel",)),
        name="in_proj",
    )(x2, ng, w_in)


def _bias_kernel(rb_ref, o_ref):
    h = pl.program_id(0)
    d = pl.program_id(1)
    t = ATT_T
    i = lax.broadcasted_iota(jnp.int32, (t, t), 0)
    j = lax.broadcasted_iota(jnp.int32, (t, t), 1)
    rel = d * t + i - j
    n = jnp.maximum(rel, 0)
    max_exact = N_BUCKETS // 2
    nf = jnp.maximum(n, 1).astype(F32)
    large = max_exact + (jnp.log(nf / max_exact) / math.log(MAX_DISTANCE / max_exact)
                         * (N_BUCKETS - max_exact)).astype(jnp.int32)
    large = jnp.minimum(large, N_BUCKETS - 1)
    bucket = jnp.where(n < max_exact, n, large)
    val = jnp.zeros((t, t), F32)
    for b in range(N_BUCKETS):
        val = jnp.where(bucket == b, rb_ref[b, h], val)
    val = (val - rb_ref[N_BUCKETS - 1, h]) * LOG2E
    val = jnp.where(rel >= 0, val, NEG_INF)
    o_ref[0:t, :] = val
    o_ref[t:2 * t, :] = val


def _bias_tiles(rel_bias):
    t = ATT_T
    return pl.pallas_call(
        _bias_kernel,
        grid=(N_HEADS, 2),
        in_specs=[pl.BlockSpec(memory_space=pltpu.SMEM)],
        out_specs=pl.BlockSpec((None, None, 2 * t, t), lambda h, d: (h, d, 0, 0)),
        out_shape=jax.ShapeDtypeStruct((N_HEADS, 2, 2 * t, t), F32),
        compiler_params=_params(("arbitrary", "arbitrary")),
        name="bias_tiles",
    )(rel_bias)


def _attn_kernel(q_ref, k_ref, v_ref, bias_ref, lam_ref, sg_ref, o_ref, vx_ref, *, lam_init):
    t = ATT_T
    s_len = k_ref.shape[0]
    vx_ref[:, 0:V_DIM] = v_ref[...]
    vx_ref[:, V_DIM:2 * V_DIM] = jnp.ones((s_len, V_DIM), BF16)
    lam = lam_ref[...]
    lam_full = (jnp.exp(jnp.sum(lam[0:1, :] * lam[1:2, :], axis=-1, keepdims=True))
                - jnp.exp(jnp.sum(lam[2:3, :] * lam[3:4, :], axis=-1, keepdims=True)) + lam_init)
    lane = lax.broadcasted_iota(jnp.int32, (t, V_DIM), 1)
    nt = (((1,), (1,)), ((), ()))

    def logits(qi):
        q = q_ref[qi * t:(qi + 1) * t, :]
        zero = jnp.zeros_like(q)
        q2 = jnp.concatenate([jnp.where(lane < HEAD_DIM, q, zero),
                              jnp.where(lane >= HEAD_DIM, q, zero)], axis=0)
        ranges = []
        if qi >= 2:
            ranges.append((0, (qi - 1) * t, None))
        if qi >= 1:
            ranges.append(((qi - 1) * t, t, 1))
        ranges.append((qi * t, t, 0))
        parts = []
        m = None
        for k0, n, bias_idx in ranges:
            sp = lax.dot_general(q2, k_ref[k0:k0 + n, :], nt, preferred_element_type=F32)
            if bias_idx is not None:
                sp = sp + bias_ref[bias_idx]
            parts.append((k0, sp))
            mp = jnp.max(sp, axis=1, keepdims=True)
            m = mp if m is None else jnp.maximum(m, mp)
        return parts, m

    order = list(reversed(range(s_len // t)))
    pending = logits(order[0])
    for idx, qi in enumerate(order):
        parts, m = pending
        if idx + 1 < len(order):
            pending = logits(order[idx + 1])
        r = None
        for k0, sp in parts:
            p = jnp.exp2((sp - m).astype(BF16))
            rp = _dot(p, vx_ref[k0:k0 + sp.shape[1], :])
            r = rp if r is None else r + rp
        o = r[:, 0:V_DIM] / r[:, V_DIM:2 * V_DIM]
        out = o[0:t, :] - lam_full * o[t:2 * t, :]
        o_ref[qi * t:(qi + 1) * t, :] = (_rms(out, sg_ref[...]) * (1.0 - lam_init)).astype(o_ref.dtype)


def _diff_attention(qkv, bias, lam, subln, lam_init):
    b, s, _ = qkv.shape
    t = ATT_T
    return pl.pallas_call(
        functools.partial(_attn_kernel, lam_init=lam_init),
        grid=(b, N_HEADS),
        in_specs=[
            pl.BlockSpec((None, s, V_DIM), lambda bi, h: (bi, 0, h)),
            pl.BlockSpec((None, s, V_DIM), lambda bi, h: (bi, 0, N_HEADS + h)),
            pl.BlockSpec((None, s, V_DIM), lambda bi, h: (bi, 0, 2 * N_HEADS + h)),
            pl.BlockSpec((None, 2, 2 * t, t), lambda bi, h: (h, 0, 0, 0)),
            pl.BlockSpec((4, HEAD_DIM), lambda bi, h: (0, 0)),
            pl.BlockSpec((1, V_DIM), lambda bi, h: (0, 0)),
        ],
        out_specs=pl.BlockSpec((None, s, V_DIM), lambda bi, h: (bi, 0, h)),
        out_shape=jax.ShapeDtypeStruct((b, s, N_HEADS * V_DIM), BF16),
        scratch_shapes=[pltpu.VMEM((s, 2 * V_DIM), BF16)],
        compiler_params=_params(("parallel", "arbitrary")),
        name="diff_attention",
    )(qkv, qkv, qkv, bias, lam, subln)


CONV_LANES = 128
CONV_ROWS = 64


def _conv_fill(z_ref, halo, sh_ref):
    sh_ref[0, 0:HALO, :] = halo
    sh_ref[0, HALO:HALO + SEQ_TS, :] = z_ref[...]


def _conv_lanes(ls, w_ref, sh_ref, cv_ref):
    ts = SEQ_TS
    ext = HALO + ts
    off = HALO - (CONV_K - 1)
    for r in range(1, SUBLANES):
        sh_ref[r, 0:ext - SUBLANES, ls] = sh_ref[0, r:r + ext - SUBLANES, ls]
    for c in range(ts // CONV_ROWS):
        acc = None
        for k in range(CONV_K):
            a, r = divmod(off + k, SUBLANES)
            row0 = c * CONV_ROWS + SUBLANES * a
            term = w_ref[k:k + 1, ls] * sh_ref[r, row0:row0 + CONV_ROWS, ls]
            acc = term if acc is None else acc + term
        cv_ref[c * CONV_ROWS:(c + 1) * CONV_ROWS, ls] = acc


def _conv_finish(cv_ref, b_ref, lg_ref, lb_ref):
    c = cv_ref[...] + b_ref[...]
    mu = jnp.mean(c, axis=-1, keepdims=True)
    xc = c - mu
    y = xc * lax.rsqrt(jnp.mean(xc * xc, axis=-1, keepdims=True) + LN_EPS) * lg_ref[...] + lb_ref[...]
    return (y * jax.nn.sigmoid(y)).astype(BF16)


def _window_sums(ext_ref, ls, w, lvl_ref):
    ext = ext_ref.shape[0]
    assert w >= 2 and w & (w - 1) == 0 and SUBLANES * (w.bit_length() - 1) <= HALO
    m, level = 1, 0
    while True:
        lo = SUBLANES * (level + 1)
        if level == 0:
            val = ext_ref[lo:ext, ls] + ext_ref[lo - m:ext - m, ls]
        else:
            src = lvl_ref.at[level % 2]
            val = src[lo:ext, :] + src[lo - m:ext - m, :]
        m, level = 2 * m, level + 1
        if m == w:
            return val[HALO - lo:, :]
        lvl_ref[level % 2, lo:ext, :] = val


def _merge_kernel(x_ref, z_ref, zh_ref, a_ref, u_ref, uh_ref, ng_ref, wg_ref,
                  cw_ref, cb_ref, clg_ref, clb_ref, pw_ref, ao_ref,
                  plw_ref, pls_ref, plo_ref, wo_ref, o_ref,
                  sh_ref, cv_ref, ext_ref, lvl_ref, pp_ref):
    ts = SEQ_TS
    i = pl.program_id(1)
    x = x_ref[...]
    h = _rms(x, ng_ref[2:3, :]).astype(BF16)

    _conv_fill(z_ref, jnp.where(i == 0, 0.0, zh_ref[...]), sh_ref)
    for c in range(0, D_MODEL, CONV_LANES):
        _conv_lanes(slice(c, c + CONV_LANES), cw_ref, sh_ref, cv_ref)
    c_act = _conv_finish(cv_ref, cb_ref, clg_ref, clb_ref)

    ext_ref[0:HALO, :] = jnp.where(i == 0, 0.0, uh_ref[...])
    ext_ref[HALO:HALO + ts, :] = u_ref[...]
    pos = i * ts + lax.broadcasted_iota(jnp.int32, (ts, 1), 0)

    def pool_group(g):
        w = POOL_WINDOWS[g]
        ls = slice(g * POOL_GD, (g + 1) * POOL_GD)
        tot = _window_sums(ext_ref, ls, w, lvl_ref)
        cnt = jnp.minimum(pos + 1, w).astype(F32)
        pooled = (tot / cnt - ext_ref[HALO:HALO + ts, ls]).astype(BF16)
        yg = _dot(pooled, plw_ref[g]) * pls_ref[:, ls]
        pp_ref[:, ls] = yg.astype(BF16)

    y_conv = _dot_cols(c_act, pw_ref)
    pool_group(0)
    y_attn = _dot_cols(a_ref[...], ao_ref)
    pool_group(1)
    merged = jax.nn.sigmoid(_dot_cols(h, wg_ref, col0=0)) * y_conv
    pool_group(2)
    merged = merged + jax.nn.sigmoid(_dot_cols(h, wg_ref, col0=D_MODEL)) * y_attn
    pool_group(3)
    gate_pool = jax.nn.sigmoid(_dot_cols(h, wg_ref, col0=2 * D_MODEL))
    y_pool = _dot_cols(pp_ref[...], plo_ref)
    merged = merged + gate_pool * y_pool
    y = _dot_cols(merged.astype(BF16), wo_ref)
    o_ref[...] = x + _rms(y, ng_ref[3:4, :])


def _merge(x, z, attn, u_pool, ng, w_in, conv_dw, conv_b, conv_lg, conv_lb, conv_pw, attn_o,
           pool_w, pool_scale, pool_o, w_out, l):
    b, s, _ = x.shape
    ts = SEQ_TS
    tile = pl.BlockSpec((None, ts, D_MODEL), lambda bi, i: (bi, i, 0))
    halo = pl.BlockSpec((None, HALO, D_MODEL),
                        lambda bi, i: (bi, jnp.maximum(i * (ts // HALO) - 1, 0), 0))
    sq = _resident((None, D_MODEL, D_MODEL), lambda bi, i: (l, 0, 0))
    vec = _resident((1, D_MODEL), lambda bi, i: (0, 0))
    return pl.pallas_call(
        _merge_kernel,
        grid=(b, s // ts),
        in_specs=[
            tile, tile, halo, tile, tile, halo,
            _resident((None, 8, D_MODEL), lambda bi, i: (l, 0, 0)),
            _resident((None, D_MODEL, 3 * D_MODEL), lambda bi, i: (l, 0, COL_GATES // (3 * D_MODEL))),
            _resident((CONV_K, D_MODEL), lambda bi, i: (0, 0)),
            vec, vec, vec,
            sq, sq,
            _resident((None, len(POOL_WINDOWS), POOL_GD, POOL_GD), lambda bi, i: (l, 0, 0, 0)),
            vec,
            sq, sq,
        ],
        out_specs=tile,
        out_shape=jax.ShapeDtypeStruct((b, s, D_MODEL), F32),
        scratch_shapes=[
            pltpu.VMEM((SUBLANES, HALO + ts, D_MODEL), F32),
            pltpu.VMEM((ts, D_MODEL), F32),
            pltpu.VMEM((HALO + ts, D_MODEL), F32),
            pltpu.VMEM((2, HALO + ts, POOL_GD), F32),
            pltpu.VMEM((ts, D_MODEL), BF16),
        ],
        compiler_params=_params(("parallel", "arbitrary")),
        name="merge",
    )(x, z, z, attn, u_pool, u_pool, ng, w_in, conv_dw, conv_b, conv_lg, conv_lb, conv_pw, attn_o,
      pool_w, pool_scale, pool_o, w_out)


def _xkv_kernel(m_ref, g_ref, w_ref, k_ref, v_ref):
    mn = _rms(m_ref[...], g_ref[...]).astype(BF16)
    k_ref[...] = _dot_cols(mn, w_ref, col0=0).astype(k_ref.dtype)
    v_ref[...] = _dot_cols(mn, w_ref, col0=D_MODEL).astype(v_ref.dtype)


def _xattn_kv(mem, g, wkv, l):
    b, m, _ = mem.shape
    out = pl.BlockSpec((None, m, D_MODEL), lambda bi: (bi, 0, 0))
    return pl.pallas_call(
        _xkv_kernel,
        grid=(b,),
        in_specs=[
            pl.BlockSpec((None, m, D_MODEL), lambda bi: (bi, 0, 0)),
            pl.BlockSpec((1, D_MODEL), lambda bi: (0, 0)),
            _resident((None, D_MODEL, 2 * D_MODEL), lambda bi: (l, 0, 0)),
        ],
        out_specs=[out, out],
        out_shape=[jax.ShapeDtypeStruct((b, m, D_MODEL), BF16)] * 2,
        compiler_params=_params(("parallel",)),
        name="xattn_kv",
    )(mem, g, wkv)


def _xattn_kernel(x_ref, k_ref, v_ref, ng_ref, wq_ref, wo_ref, o_ref, oh_ref):
    x = x_ref[...]
    h = _rms(x, ng_ref[4:5, :]).astype(BF16)
    q = (_dot_cols(h, wq_ref) * (X_HEAD_DIM ** -0.5)).astype(BF16)
    heads = [slice(hd * X_HEAD_DIM, (hd + 1) * X_HEAD_DIM) for hd in range(X_HEADS)]
    logits = [lax.dot_general(q[:, ls], k_ref[:, ls], (((1,), (1,)), ((), ())),
                              preferred_element_type=F32) for ls in heads]
    for ls, s in zip(heads, logits):
        p = jnp.exp(s - jnp.max(s, axis=-1, keepdims=True))
        denom = jnp.sum(p, axis=-1, keepdims=True)
        oh_ref[:, ls] = (_dot(p.astype(BF16), v_ref[:, ls]) / denom).astype(BF16)
    y = _dot_cols(oh_ref[...], wo_ref)
    o_ref[...] = x + _rms(y, ng_ref[5:6, :])


def _xattn(x, k, v, ng, wq, wo, l):
    b, s, _ = x.shape
    m = k.shape[1]
    ts = XATT_TS
    tile = pl.BlockSpec((None, ts, D_MODEL), lambda bi, i: (bi, i, 0))
    kv = pl.BlockSpec((None, m, D_MODEL), lambda bi, i: (bi, 0, 0))
    sq = _resident((None, D_MODEL, D_MODEL), lambda bi, i: (l, 0, 0))
    return pl.pallas_call(
        _xattn_kernel,
        grid=(b, s // ts),
        in_specs=[tile, kv, kv, _resident((None, 8, D_MODEL), lambda bi, i: (l, 0, 0)), sq, sq],
        out_specs=tile,
        out_shape=jax.ShapeDtypeStruct((b, s, D_MODEL), F32),
        scratch_shapes=[pltpu.VMEM((ts, D_MODEL), BF16)],
        compiler_params=_params(("parallel", "arbitrary")),
        name="xattn",
    )(x, k, v, ng, wq, wo)


def kernel(x, mem, norm_g, ffn_w1, ffn_w3, ffn_w2, w_in, conv_dw, conv_dw_b, conv_ln_g, conv_ln_b, conv_pw, attn_lam, attn_subln, attn_o, rel_bias, pool_w, pool_scale, pool_o, w_out, mem_norm, xattn_q, xattn_kv, xattn_o):
    b, s, d = x.shape
    n = b * s
    w1, w3, w2 = ffn_w1.astype(BF16), ffn_w3.astype(BF16), ffn_w2.astype(BF16)
    w_in_b = w_in.astype(BF16)
    conv_pw_b, attn_o_b, pool_w_b = conv_pw.astype(BF16), attn_o.astype(BF16), pool_w.astype(BF16)
    pool_o_b, w_out_b = pool_o.astype(BF16), w_out.astype(BF16)
    xq_b, xkv_b, xo_b = xattn_q.astype(BF16), xattn_kv.astype(BF16), xattn_o.astype(BF16)
    bias = _bias_tiles(rel_bias)

    for l in range(DEPTH):
        lam_init = 0.8 - 0.6 * math.exp(-0.3 * l)
        x2 = _ffn(x.reshape(n, d), norm_g, w1, w3, w2, l, 0)
        z, qkv, u_pool = _in_proj(x2, norm_g, w_in_b, l)
        attn = _diff_attention(qkv.reshape(b, s, 3 * d), bias, attn_lam[l], attn_subln[l][None, :],
                               lam_init)
        x = _merge(x2.reshape(b, s, d), z.reshape(b, s, d), attn, u_pool.reshape(b, s, d), norm_g,
                   w_in_b, conv_dw[l], conv_dw_b[l][None, :], conv_ln_g[l][None, :],
                   conv_ln_b[l][None, :], conv_pw_b, attn_o_b, pool_w_b, pool_scale[l][None, :],
                   pool_o_b, w_out_b, l)
        xk, xv = _xattn_kv(mem, mem_norm[l][None, :], xkv_b, l)
        x = _xattn(x, xk, xv, norm_g, xq_b, xo_b, l)
        x = _ffn(x.reshape(n, d), norm_g, w1, w3, w2, l, 1).reshape(b, s, d)
    return x
```

```python
import functools
import math

import jax
import jax.numpy as jnp
from jax import lax
from jax.experimental import pallas as pl
from jax.experimental.pallas import tpu as pltpu

D_MODEL = 1024
DEPTH = 4
D_FF = 2816
CONV_K = 31
N_HEADS = 8
HEAD_DIM = 64
V_DIM = 2 * HEAD_DIM
N_BUCKETS = 32
MAX_DISTANCE = 128
POOL_WINDOWS = (2, 4, 8, 16)
POOL_GD = D_MODEL // len(POOL_WINDOWS)
X_HEADS = 4
X_HEAD_DIM = D_MODEL // X_HEADS
COL_CONV_A = 0
COL_CONV_G = D_MODEL
COL_QKV = 2 * D_MODEL
COL_POOL = 5 * D_MODEL
COL_GATES = 6 * D_MODEL
RMS_EPS = 1e-6
LN_EPS = 1e-5
NEG_INF = -1e30
LOG2E = math.log2(math.e)

F32 = jnp.float32
BF16 = jnp.bfloat16

MXU_N = 256
SUBLANES = 8
HALO = 32
FFN_TM = 512
FFN_TF = 256
PROJ_TM = 512
SEQ_TS = 256
XATT_TS = 512
ATT_T = 256
VMEM_LIMIT = 56 * 1024 * 1024


def _rms(x, g):
    ms = jnp.mean(x * x, axis=-1, keepdims=True)
    return x * lax.rsqrt(ms + RMS_EPS) * g


def _dot(a, b):
    return jnp.dot(a, b, preferred_element_type=F32)


def _dot_cols(a, w_ref, rows=slice(None), col0=0, ncols=D_MODEL):
    return jnp.concatenate(
        [_dot(a, w_ref[rows, c:c + MXU_N]) for c in range(col0, col0 + ncols, MXU_N)], axis=1)


def _params(sem):
    return pltpu.CompilerParams(dimension_semantics=sem, vmem_limit_bytes=VMEM_LIMIT)


def _resident(shape, index_map):
    return pl.BlockSpec(shape, index_map, pipeline_mode=pl.Buffered(1))


FFN_CAST_ROWS = 128


def _ffn_kernel(x_ref, ng_ref, w1_ref, w3_ref, w2_ref, *rest, g_in, g_out, cast_next):
    if cast_next:
        w1f_ref, w3f_ref, w2f_ref, o_ref, w1n_ref, w3n_ref, w2n_ref = rest
        w1n_ref[...] = w1f_ref[...].astype(BF16)
        w3n_ref[...] = w3f_ref[...].astype(BF16)
        w2n_ref[...] = w2f_ref[...].astype(BF16)
    else:
        (o_ref,) = rest
    x = x_ref[...]
    h = _rms(x, ng_ref[g_in:g_in + 1, :]).astype(BF16)
    acc = None
    for c in range(D_FF // FFN_TF):
        sl = slice(c * FFN_TF, (c + 1) * FFN_TF)
        a = _dot(h, w1_ref[:, sl])
        b = _dot(h, w3_ref[:, sl])
        t = (a * jax.nn.sigmoid(a) * b).astype(BF16)
        y = _dot_cols(t, w2_ref, rows=sl)
        acc = y if acc is None else acc + y
    o_ref[...] = x + 0.5 * _rms(acc, ng_ref[g_out:g_out + 1, :])


def _ffn(x2, ng, w1, w3, w2, l, s, cast_next=None):
    n = x2.shape[0]
    steps = n // FFN_TM
    g_in, g_out = (0, 1) if s == 0 else (6, 7)
    row = pl.BlockSpec((FFN_TM, D_MODEL), lambda i: (i, 0))
    in_specs = [
        row,
        _resident((None, 8, D_MODEL), lambda i: (l, 0, 0)),
        _resident((D_MODEL, D_FF), lambda i: (0, 0)),
        _resident((D_MODEL, D_FF), lambda i: (0, 0)),
        _resident((D_FF, D_MODEL), lambda i: (0, 0)),
    ]
    out_specs = [row]
    out_shape = [jax.ShapeDtypeStruct((n, D_MODEL), F32)]
    args = [x2, ng, w1, w3, w2]
    if cast_next is not None:
        w1f, w3f, w2f, l2, s2 = cast_next
        up_rows = D_MODEL // steps
        dn_blocks = D_FF // FFN_CAST_ROWS
        assert D_MODEL % steps == 0 and up_rows % 16 == 0
        assert D_FF % FFN_CAST_ROWS == 0 and dn_blocks <= steps
        in_specs += [
            pl.BlockSpec((None, None, up_rows, D_FF), lambda i: (l2, s2, i, 0)),
            pl.BlockSpec((None, None, up_rows, D_FF), lambda i: (l2, s2, i, 0)),
            pl.BlockSpec((None, None, FFN_CAST_ROWS, D_MODEL),
                         lambda i: (l2, s2, jnp.minimum(i, dn_blocks - 1), 0)),
        ]
        out_specs += [
            pl.BlockSpec((up_rows, D_FF), lambda i: (i, 0)),
            pl.BlockSpec((up_rows, D_FF), lambda i: (i, 0)),
            pl.BlockSpec((FFN_CAST_ROWS, D_MODEL), lambda i: (jnp.minimum(i, dn_blocks - 1), 0)),
        ]
        out_shape += [jax.ShapeDtypeStruct((D_MODEL, D_FF), BF16),
                      jax.ShapeDtypeStruct((D_MODEL, D_FF), BF16),
                      jax.ShapeDtypeStruct((D_FF, D_MODEL), BF16)]
        args += [w1f, w3f, w2f]
    return pl.pallas_call(
        functools.partial(_ffn_kernel, g_in=g_in, g_out=g_out, cast_next=cast_next is not None),
        grid=(steps,),
        in_specs=in_specs,
        out_specs=out_specs,
        out_shape=out_shape,
        compiler_params=_params(("arbitrary",)),
        name="ffn",
    )(*args)


def _in_proj_kernel(x_ref, ng_ref, w_ref, z_ref, qkv_ref, u_ref):
    d = D_MODEL
    h = _rms(x_ref[...], ng_ref[2:3, :]).astype(BF16)
    a = _dot_cols(h, w_ref, col0=COL_CONV_A)
    gt = _dot_cols(h, w_ref, col0=COL_CONV_G)
    z_ref[...] = a * jax.nn.sigmoid(gt)
    q = _dot_cols(h, w_ref, col0=COL_QKV) * (HEAD_DIM ** -0.5 * LOG2E)
    qkv_ref[:, 0:d] = q.astype(BF16)
    qkv_ref[:, d:3 * d] = _dot_cols(h, w_ref, col0=COL_QKV + d, ncols=2 * d).astype(BF16)
    u_ref[...] = _dot_cols(h, w_ref, col0=COL_POOL)


def _in_proj(x2, ng, w_in, l):
    n = x2.shape[0]
    row = pl.BlockSpec((PROJ_TM, D_MODEL), lambda i: (i, 0))
    return pl.pallas_call(
        _in_proj_kernel,
        grid=(n // PROJ_TM,),
        in_specs=[
            row,
            _resident((None, 8, D_MODEL), lambda i: (l, 0, 0)),
            _resident((None, D_MODEL, COL_GATES), lambda i: (l, 0, 0)),
        ],
        out_specs=[row, pl.BlockSpec((PROJ_TM, 3 * D_MODEL), lambda i: (i, 0)), row],
        out_shape=[jax.ShapeDtypeStruct((n, D_MODEL), F32),
                   jax.ShapeDtypeStruct((n, 3 * D_MODEL), BF16),
                   jax.ShapeDtypeStruct((n, D_MODEL), F32)],
        compiler_params=_params(("parallel",)),
        name="in_proj",
    )(x2, ng, w_in)


def _bias_kernel(rb_ref, o_ref):
    h = pl.program_id(0)
    d = pl.program_id(1)
    t = ATT_T
    i = lax.broadcasted_iota(jnp.int32, (t, t), 0)
    j = lax.broadcasted_iota(jnp.int32, (t, t), 1)
    rel = d * t + i - j
    n = jnp.maximum(rel, 0)
    max_exact = N_BUCKETS // 2
    nf = jnp.maximum(n, 1).astype(F32)
    large = max_exact + (jnp.log(nf / max_exact) / math.log(MAX_DISTANCE / max_exact)
                         * (N_BUCKETS - max_exact)).astype(jnp.int32)
    large = jnp.minimum(large, N_BUCKETS - 1)
    bucket = jnp.where(n < max_exact, n, large)
    val = jnp.zeros((t, t), F32)
    for b in range(N_BUCKETS):
        val = jnp.where(bucket == b, rb_ref[b, h], val)
    val = (val - rb_ref[N_BUCKETS - 1, h]) * LOG2E
    val = jnp.where(rel >= 0, val, NEG_INF)
    o_ref[0:t, :] = val
    o_ref[t:2 * t, :] = val


def _bias_tiles(rel_bias):
    t = ATT_T
    return pl.pallas_call(
        _bias_kernel,
        grid=(N_HEADS, 2),
        in_specs=[pl.BlockSpec(memory_space=pltpu.SMEM)],
        out_specs=pl.BlockSpec((None, None, 2 * t, t), lambda h, d: (h, d, 0, 0)),
        out_shape=jax.ShapeDtypeStruct((N_HEADS, 2, 2 * t, t), F32),
        compiler_params=_params(("arbitrary", "arbitrary")),
        name="bias_tiles",
    )(rel_bias)


def _attn_kernel(q_ref, k_ref, v_ref, bias_ref, lam_ref, sg_ref, o_ref, vx_ref, *, lam_init):
    t = ATT_T
    s_len = k_ref.shape[0]
    vx_ref[:, 0:V_DIM] = v_ref[...]
    vx_ref[:, V_DIM:2 * V_DIM] = jnp.ones((s_len, V_DIM), BF16)
    lam = lam_ref[...]
    lam_full = (jnp.exp(jnp.sum(lam[0:1, :] * lam[1:2, :], axis=-1, keepdims=True))
                - jnp.exp(jnp.sum(lam[2:3, :] * lam[3:4, :], axis=-1, keepdims=True)) + lam_init)
    lane = lax.broadcasted_iota(jnp.int32, (t, V_DIM), 1)
    nt = (((1,), (1,)), ((), ()))

    def logits(qi):
        q = q_ref[qi * t:(qi + 1) * t, :]
        zero = jnp.zeros_like(q)
        q2 = jnp.concatenate([jnp.where(lane < HEAD_DIM, q, zero),
                              jnp.where(lane >= HEAD_DIM, q, zero)], axis=0)
        ranges = []
        if qi >= 2:
            ranges.append((0, (qi - 1) * t, None))
        if qi >= 1:
            ranges.append(((qi - 1) * t, t, 1))
        ranges.append((qi * t, t, 0))
        parts = []
        m = None
        for k0, n, bias_idx in ranges:
            sp = lax.dot_general(q2, k_ref[k0:k0 + n, :], nt, preferred_element_type=F32)
            if bias_idx is not None:
                sp = sp + bias_ref[bias_idx]
            parts.append((k0, sp))
            mp = jnp.max(sp, axis=1, keepdims=True)
            m = mp if m is None else jnp.maximum(m, mp)
        return parts, m

    order = list(reversed(range(s_len // t)))
    pending = logits(order[0])
    for idx, qi in enumerate(order):
        parts, m = pending
        if idx + 1 < len(order):
            pending = logits(order[idx + 1])
        r = None
        for k0, sp in parts:
            p = jnp.exp2((sp - m).astype(BF16))
            rp = _dot(p, vx_ref[k0:k0 + sp.shape[1], :])
            r = rp if r is None else r + rp
        o = r[:, 0:V_DIM] / r[:, V_DIM:2 * V_DIM]
        out = o[0:t, :] - lam_full * o[t:2 * t, :]
        o_ref[qi * t:(qi + 1) * t, :] = (_rms(out, sg_ref[...]) * (1.0 - lam_init)).astype(o_ref.dtype)


def _diff_attention(qkv, bias, lam, subln, lam_init):
    b, s, _ = qkv.shape
    t = ATT_T
    return pl.pallas_call(
        functools.partial(_attn_kernel, lam_init=lam_init),
        grid=(b, N_HEADS),
        in_specs=[
            pl.BlockSpec((None, s, V_DIM), lambda bi, h: (bi, 0, h)),
            pl.BlockSpec((None, s, V_DIM), lambda bi, h: (bi, 0, N_HEADS + h)),
            pl.BlockSpec((None, s, V_DIM), lambda bi, h: (bi, 0, 2 * N_HEADS + h)),
            pl.BlockSpec((None, 2, 2 * t, t), lambda bi, h: (h, 0, 0, 0)),
            pl.BlockSpec((4, HEAD_DIM), lambda bi, h: (0, 0)),
            pl.BlockSpec((1, V_DIM), lambda bi, h: (0, 0)),
        ],
        out_specs=pl.BlockSpec((None, s, V_DIM), lambda bi, h: (bi, 0, h)),
        out_shape=jax.ShapeDtypeStruct((b, s, N_HEADS * V_DIM), BF16),
        scratch_shapes=[pltpu.VMEM((s, 2 * V_DIM), BF16)],
        compiler_params=_params(("parallel", "arbitrary")),
        name="diff_attention",
    )(qkv, qkv, qkv, bias, lam, subln)


CONV_LANES = 128
CONV_ROWS = 64


def _conv_fill(z_ref, halo, sh_ref):
    sh_ref[0, 0:HALO, :] = halo
    sh_ref[0, HALO:HALO + SEQ_TS, :] = z_ref[...]


def _conv_lanes(ls, w_ref, sh_ref, cv_ref):
    ts = SEQ_TS
    ext = HALO + ts
    off = HALO - (CONV_K - 1)
    for r in range(1, SUBLANES):
        sh_ref[r, 0:ext - SUBLANES, ls] = sh_ref[0, r:r + ext - SUBLANES, ls]
    for c in range(ts // CONV_ROWS):
        acc = None
        for k in range(CONV_K):
            a, r = divmod(off + k, SUBLANES)
            row0 = c * CONV_ROWS + SUBLANES * a
            term = w_ref[k:k + 1, ls] * sh_ref[r, row0:row0 + CONV_ROWS, ls]
            acc = term if acc is None else acc + term
        cv_ref[c * CONV_ROWS:(c + 1) * CONV_ROWS, ls] = acc


def _conv_finish(cv_ref, b_ref, lg_ref, lb_ref):
    c = cv_ref[...] + b_ref[...]
    mu = jnp.mean(c, axis=-1, keepdims=True)
    xc = c - mu
    y = xc * lax.rsqrt(jnp.mean(xc * xc, axis=-1, keepdims=True) + LN_EPS) * lg_ref[...] + lb_ref[...]
    return (y * jax.nn.sigmoid(y)).astype(BF16)


def _window_sums(ext_ref, ls, w, lvl_ref):
    ext = ext_ref.shape[0]
    assert w >= 2 and w & (w - 1) == 0 and SUBLANES * (w.bit_length() - 1) <= HALO
    m, level = 1, 0
    while True:
        lo = SUBLANES * (level + 1)
        if level == 0:
            val = ext_ref[lo:ext, ls] + ext_ref[lo - m:ext - m, ls]
        else:
            src = lvl_ref.at[level % 2]
            val = src[lo:ext, :] + src[lo - m:ext - m, :]
        m, level = 2 * m, level + 1
        if m == w:
            return val[HALO - lo:, :]
        lvl_ref[level % 2, lo:ext, :] = val


def _merge_kernel(x_ref, z_ref, zh_ref, a_ref, u_ref, uh_ref, ng_ref, wg_ref,
                  cw_ref, cb_ref, clg_ref, clb_ref, pw_ref, ao_ref,
                  plw_ref, pls_ref, plo_ref, wo_ref, o_ref,
                  sh_ref, cv_ref, ext_ref, lvl_ref, pp_ref):
    ts = SEQ_TS
    i = pl.program_id(1)
    x = x_ref[...]
    h = _rms(x, ng_ref[2:3, :]).astype(BF16)

    _conv_fill(z_ref, jnp.where(i == 0, 0.0, zh_ref[...]), sh_ref)
    for c in range(0, D_MODEL, CONV_LANES):
        _conv_lanes(slice(c, c + CONV_LANES), cw_ref, sh_ref, cv_ref)
    c_act = _conv_finish(cv_ref, cb_ref, clg_ref, clb_ref)

    ext_ref[0:HALO, :] = jnp.where(i == 0, 0.0, uh_ref[...])
    ext_ref[HALO:HALO + ts, :] = u_ref[...]
    pos = i * ts + lax.broadcasted_iota(jnp.int32, (ts, 1), 0)

    def pool_group(g):
        w = POOL_WINDOWS[g]
        ls = slice(g * POOL_GD, (g + 1) * POOL_GD)
        tot = _window_sums(ext_ref, ls, w, lvl_ref)
        cnt = jnp.minimum(pos + 1, w).astype(F32)
        pooled = (tot / cnt - ext_ref[HALO:HALO + ts, ls]).astype(BF16)
        yg = _dot(pooled, plw_ref[g]) * pls_ref[:, ls]
        pp_ref[:, ls] = yg.astype(BF16)

    y_conv = _dot_cols(c_act, pw_ref)
    pool_group(0)
    y_attn = _dot_cols(a_ref[...], ao_ref)
    pool_group(1)
    merged = jax.nn.sigmoid(_dot_cols(h, wg_ref, col0=0)) * y_conv
    pool_group(2)
    merged = merged + jax.nn.sigmoid(_dot_cols(h, wg_ref, col0=D_MODEL)) * y_attn
    pool_group(3)
    gate_pool = jax.nn.sigmoid(_dot_cols(h, wg_ref, col0=2 * D_MODEL))
    y_pool = _dot_cols(pp_ref[...], plo_ref)
    merged = merged + gate_pool * y_pool
    y = _dot_cols(merged.astype(BF16), wo_ref)
    o_ref[...] = x + _rms(y, ng_ref[3:4, :])


def _merge(x, z, attn, u_pool, ng, w_in, conv_dw, conv_b, conv_lg, conv_lb, conv_pw, attn_o,
           pool_w, pool_scale, pool_o, w_out, l):
    b, s, _ = x.shape
    ts = SEQ_TS
    tile = pl.BlockSpec((None, ts, D_MODEL), lambda bi, i: (bi, i, 0))
    halo = pl.BlockSpec((None, HALO, D_MODEL),
                        lambda bi, i: (bi, jnp.maximum(i * (ts // HALO) - 1, 0), 0))
    sq = _resident((None, D_MODEL, D_MODEL), lambda bi, i: (l, 0, 0))
    vec = _resident((1, D_MODEL), lambda bi, i: (0, 0))
    return pl.pallas_call(
        _merge_kernel,
        grid=(b, s // ts),
        in_specs=[
            tile, tile, halo, tile, tile, halo,
            _resident((None, 8, D_MODEL), lambda bi, i: (l, 0, 0)),
            _resident((None, D_MODEL, 3 * D_MODEL), lambda bi, i: (l, 0, COL_GATES // (3 * D_MODEL))),
            _resident((CONV_K, D_MODEL), lambda bi, i: (0, 0)),
            vec, vec, vec,
            sq, sq,
            _resident((None, len(POOL_WINDOWS), POOL_GD, POOL_GD), lambda bi, i: (l, 0, 0, 0)),
            vec,
            sq, sq,
        ],
        out_specs=tile,
        out_shape=jax.ShapeDtypeStruct((b, s, D_MODEL), F32),
        scratch_shapes=[
            pltpu.VMEM((SUBLANES, HALO + ts, D_MODEL), F32),
            pltpu.VMEM((ts, D_MODEL), F32),
            pltpu.VMEM((HALO + ts, D_MODEL), F32),
            pltpu.VMEM((2, HALO + ts, POOL_GD), F32),
            pltpu.VMEM((ts, D_MODEL), BF16),
        ],
        compiler_params=_params(("parallel", "arbitrary")),
        name="merge",
    )(x, z, z, attn, u_pool, u_pool, ng, w_in, conv_dw, conv_b, conv_lg, conv_lb, conv_pw, attn_o,
      pool_w, pool_scale, pool_o, w_out)


def _xkv_kernel(m_ref, g_ref, w_ref, k_ref, v_ref):
    mn = _rms(m_ref[...], g_ref[...]).astype(BF16)
    k_ref[...] = _dot_cols(mn, w_ref, col0=0).astype(k_ref.dtype)
    v_ref[...] = _dot_cols(mn, w_ref, col0=D_MODEL).astype(v_ref.dtype)


def _xattn_kv(mem, g, wkv, l):
    b, m, _ = mem.shape
    out = pl.BlockSpec((None, m, D_MODEL), lambda bi: (bi, 0, 0))
    return pl.pallas_call(
        _xkv_kernel,
        grid=(b,),
        in_specs=[
            pl.BlockSpec((None, m, D_MODEL), lambda bi: (bi, 0, 0)),
            pl.BlockSpec((1, D_MODEL), lambda bi: (0, 0)),
            _resident((None, D_MODEL, 2 * D_MODEL), lambda bi: (l, 0, 0)),
        ],
        out_specs=[out, out],
        out_shape=[jax.ShapeDtypeStruct((b, m, D_MODEL), BF16)] * 2,
        compiler_params=_params(("parallel",)),
        name="xattn_kv",
    )(mem, g, wkv)


def _xattn_kernel(x_ref, k_ref, v_ref, ng_ref, wq_ref, wo_ref, o_ref, oh_ref):
    x = x_ref[...]
    h = _rms(x, ng_ref[4:5, :]).astype(BF16)
    q = (_dot_cols(h, wq_ref) * (X_HEAD_DIM ** -0.5)).astype(BF16)
    heads = [slice(hd * X_HEAD_DIM, (hd + 1) * X_HEAD_DIM) for hd in range(X_HEADS)]
    logits = [lax.dot_general(q[:, ls], k_ref[:, ls], (((1,), (1,)), ((), ())),
                              preferred_element_type=F32) for ls in heads]
    for ls, s in zip(heads, logits):
        p = jnp.exp(s - jnp.max(s, axis=-1, keepdims=True))
        denom = jnp.sum(p, axis=-1, keepdims=True)
        oh_ref[:, ls] = (_dot(p.astype(BF16), v_ref[:, ls]) / denom).astype(BF16)
    y = _dot_cols(oh_ref[...], wo_ref)
    o_ref[...] = x + _rms(y, ng_ref[5:6, :])


def _xattn(x, k, v, ng, wq, wo, l):
    b, s, _ = x.shape
    m = k.shape[1]
    ts = XATT_TS
    tile = pl.BlockSpec((None, ts, D_MODEL), lambda bi, i: (bi, i, 0))
    kv = pl.BlockSpec((None, m, D_MODEL), lambda bi, i: (bi, 0, 0))
    sq = _resident((None, D_MODEL, D_MODEL), lambda bi, i: (l, 0, 0))
    return pl.pallas_call(
        _xattn_kernel,
        grid=(b, s // ts),
        in_specs=[tile, kv, kv, _resident((None, 8, D_MODEL), lambda bi, i: (l, 0, 0)), sq, sq],
        out_specs=tile,
        out_shape=jax.ShapeDtypeStruct((b, s, D_MODEL), F32),
        scratch_shapes=[pltpu.VMEM((ts, D_MODEL), BF16)],
        compiler_params=_params(("parallel", "arbitrary")),
        name="xattn",
    )(x, k, v, ng, wq, wo)


def kernel(x, mem, norm_g, ffn_w1, ffn_w3, ffn_w2, w_in, conv_dw, conv_dw_b, conv_ln_g, conv_ln_b, conv_pw, attn_lam, attn_subln, attn_o, rel_bias, pool_w, pool_scale, pool_o, w_out, mem_norm, xattn_q, xattn_kv, xattn_o):
    b, s, d = x.shape
    n = b * s
    ffn_w = (ffn_w1[0, 0].astype(BF16), ffn_w3[0, 0].astype(BF16), ffn_w2[0, 0].astype(BF16))
    w_in_b = w_in.astype(BF16)
    conv_pw_b, attn_o_b, pool_w_b = conv_pw.astype(BF16), attn_o.astype(BF16), pool_w.astype(BF16)
    pool_o_b, w_out_b = pool_o.astype(BF16), w_out.astype(BF16)
    xq_b, xkv_b, xo_b = xattn_q.astype(BF16), xattn_kv.astype(BF16), xattn_o.astype(BF16)
    bias = _bias_tiles(rel_bias)

    for l in range(DEPTH):
        lam_init = 0.8 - 0.6 * math.exp(-0.3 * l)
        x2, *ffn_w = _ffn(x.reshape(n, d), norm_g, *ffn_w, l, 0,
                          cast_next=(ffn_w1, ffn_w3, ffn_w2, l, 1))
        z, qkv, u_pool = _in_proj(x2, norm_g, w_in_b, l)
        attn = _diff_attention(qkv.reshape(b, s, 3 * d), bias, attn_lam[l], attn_subln[l][None, :],
                               lam_init)
        x = _merge(x2.reshape(b, s, d), z.reshape(b, s, d), attn, u_pool.reshape(b, s, d), norm_g,
                   w_in_b, conv_dw[l], conv_dw_b[l][None, :], conv_ln_g[l][None, :],
                   conv_ln_b[l][None, :], conv_pw_b, attn_o_b, pool_w_b, pool_scale[l][None, :],
                   pool_o_b, w_out_b, l)
        xk, xv = _xattn_kv(mem, mem_norm[l][None, :], xkv_b, l)
        x = _xattn(x, xk, xv, norm_g, xq_b, xo_b, l)
        nxt = (ffn_w1, ffn_w3, ffn_w2, l + 1, 0) if l + 1 < DEPTH else None
        x, *ffn_w = _ffn(x.reshape(n, d), norm_g, *ffn_w, l, 1, cast_next=nxt)
        x = x.reshape(b, s, d)
    return x
```

```python
import functools
import math

import jax
import jax.numpy as jnp
from jax import lax
from jax.experimental import pallas as pl
from jax.experimental.pallas import tpu as pltpu

D_MODEL = 1024
DEPTH = 4
D_FF = 2816
CONV_K = 31
N_HEADS = 8
HEAD_DIM = 64
V_DIM = 2 * HEAD_DIM
N_BUCKETS = 32
MAX_DISTANCE = 128
POOL_WINDOWS = (2, 4, 8, 16)
POOL_GD = D_MODEL // len(POOL_WINDOWS)
X_HEADS = 4
X_HEAD_DIM = D_MODEL // X_HEADS
COL_CONV_A = 0
COL_CONV_G = D_MODEL
COL_QKV = 2 * D_MODEL
COL_POOL = 5 * D_MODEL
COL_GATES = 6 * D_MODEL
RMS_EPS = 1e-6
LN_EPS = 1e-5
NEG_INF = -1e30
LOG2E = math.log2(math.e)

F32 = jnp.float32
BF16 = jnp.bfloat16

MXU_N = 256
SUBLANES = 8
HALO = 32
FFN_TM = 512
FFN_TF = 256
PROJ_TM = 512
SEQ_TS = 256
XATT_TS = 512
ATT_T = 256
VMEM_LIMIT = 56 * 1024 * 1024


def _rms(x, g):
    ms = jnp.mean(x * x, axis=-1, keepdims=True)
    return x * lax.rsqrt(ms + RMS_EPS) * g


def _dot(a, b):
    return jnp.dot(a, b, preferred_element_type=F32)


def _dot_cols(a, w_ref, rows=slice(None), col0=0, ncols=D_MODEL):
    return jnp.concatenate(
        [_dot(a, w_ref[rows, c:c + MXU_N]) for c in range(col0, col0 + ncols, MXU_N)], axis=1)


def _params(sem):
    return pltpu.CompilerParams(dimension_semantics=sem, vmem_limit_bytes=VMEM_LIMIT)


def _resident(shape, index_map):
    return pl.BlockSpec(shape, index_map, pipeline_mode=pl.Buffered(1))


FFN_CAST_ROWS = 128


def _ffn_kernel(x_ref, ng_ref, w1_ref, w3_ref, w2_ref, *rest, g_in, g_out, n_cast):
    src_refs, o_ref, dst_refs = rest[:n_cast], rest[n_cast], rest[n_cast + 1:]
    for src_ref, dst_ref in zip(src_refs, dst_refs):
        dst_ref[...] = src_ref[...].astype(BF16)
    x = x_ref[...]
    h = _rms(x, ng_ref[g_in:g_in + 1, :]).astype(BF16)
    acc = None
    for c in range(D_FF // FFN_TF):
        sl = slice(c * FFN_TF, (c + 1) * FFN_TF)
        a = _dot(h, w1_ref[:, sl])
        b = _dot(h, w3_ref[:, sl])
        t = (a * jax.nn.sigmoid(a) * b).astype(BF16)
        y = _dot_cols(t, w2_ref, rows=sl)
        acc = y if acc is None else acc + y
    o_ref[...] = x + 0.5 * _rms(acc, ng_ref[g_out:g_out + 1, :])


def _ffn(x2, ng, w1, w3, w2, l, s, cast_next=None, cast_layer=None):
    n = x2.shape[0]
    steps = n // FFN_TM
    up_rows = D_MODEL // steps
    assert D_MODEL % steps == 0 and up_rows % 16 == 0
    g_in, g_out = (0, 1) if s == 0 else (6, 7)
    row = pl.BlockSpec((FFN_TM, D_MODEL), lambda i: (i, 0))
    in_specs = [
        row,
        _resident((None, 8, D_MODEL), lambda i: (l, 0, 0)),
        _resident((D_MODEL, D_FF), lambda i: (0, 0)),
        _resident((D_MODEL, D_FF), lambda i: (0, 0)),
        _resident((D_FF, D_MODEL), lambda i: (0, 0)),
    ]
    out_specs = [row]
    out_shape = [jax.ShapeDtypeStruct((n, D_MODEL), F32)]
    args = [x2, ng, w1, w3, w2]
    if cast_next is not None:
        w1f, w3f, w2f, l2, s2 = cast_next
        dn_blocks = D_FF // FFN_CAST_ROWS
        assert D_FF % FFN_CAST_ROWS == 0 and dn_blocks <= steps
        in_specs += [
            pl.BlockSpec((None, None, up_rows, D_FF), lambda i: (l2, s2, i, 0)),
            pl.BlockSpec((None, None, up_rows, D_FF), lambda i: (l2, s2, i, 0)),
            pl.BlockSpec((None, None, FFN_CAST_ROWS, D_MODEL),
                         lambda i: (l2, s2, jnp.minimum(i, dn_blocks - 1), 0)),
        ]
        out_specs += [
            pl.BlockSpec((up_rows, D_FF), lambda i: (i, 0)),
            pl.BlockSpec((up_rows, D_FF), lambda i: (i, 0)),
            pl.BlockSpec((FFN_CAST_ROWS, D_MODEL), lambda i: (jnp.minimum(i, dn_blocks - 1), 0)),
        ]
        out_shape += [jax.ShapeDtypeStruct((D_MODEL, D_FF), BF16),
                      jax.ShapeDtypeStruct((D_MODEL, D_FF), BF16),
                      jax.ShapeDtypeStruct((D_FF, D_MODEL), BF16)]
        args += [w1f, w3f, w2f]
    if cast_layer is not None:
        arrays, l3 = cast_layer
        for arr in arrays:
            cols = arr.shape[-1]
            assert arr.ndim == 3 and arr.shape[1] == D_MODEL
            in_specs.append(pl.BlockSpec((None, up_rows, cols), lambda i: (l3, i, 0)))
            out_specs.append(pl.BlockSpec((up_rows, cols), lambda i: (i, 0)))
            out_shape.append(jax.ShapeDtypeStruct((D_MODEL, cols), BF16))
            args.append(arr)
    return pl.pallas_call(
        functools.partial(_ffn_kernel, g_in=g_in, g_out=g_out, n_cast=len(args) - 5),
        grid=(steps,),
        in_specs=in_specs,
        out_specs=out_specs,
        out_shape=out_shape,
        compiler_params=_params(("arbitrary",)),
        name="ffn",
    )(*args)


def _in_proj_kernel(x_ref, ng_ref, w_ref, z_ref, qkv_ref, u_ref):
    d = D_MODEL
    h = _rms(x_ref[...], ng_ref[2:3, :]).astype(BF16)
    a = _dot_cols(h, w_ref, col0=COL_CONV_A)
    gt = _dot_cols(h, w_ref, col0=COL_CONV_G)
    z_ref[...] = a * jax.nn.sigmoid(gt)
    q = _dot_cols(h, w_ref, col0=COL_QKV) * (HEAD_DIM ** -0.5 * LOG2E)
    qkv_ref[:, 0:d] = q.astype(BF16)
    qkv_ref[:, d:3 * d] = _dot_cols(h, w_ref, col0=COL_QKV + d, ncols=2 * d).astype(BF16)
    u_ref[...] = _dot_cols(h, w_ref, col0=COL_POOL)


def _in_proj(x2, ng, w_in, l):
    n = x2.shape[0]
    row = pl.BlockSpec((PROJ_TM, D_MODEL), lambda i: (i, 0))
    return pl.pallas_call(
        _in_proj_kernel,
        grid=(n // PROJ_TM,),
        in_specs=[
            row,
            _resident((None, 8, D_MODEL), lambda i: (l, 0, 0)),
            _resident((D_MODEL, COL_GATES), lambda i: (0, 0)),
        ],
        out_specs=[row, pl.BlockSpec((PROJ_TM, 3 * D_MODEL), lambda i: (i, 0)), row],
        out_shape=[jax.ShapeDtypeStruct((n, D_MODEL), F32),
                   jax.ShapeDtypeStruct((n, 3 * D_MODEL), BF16),
                   jax.ShapeDtypeStruct((n, D_MODEL), F32)],
        compiler_params=_params(("parallel",)),
        name="in_proj",
    )(x2, ng, w_in)


def _bias_kernel(rb_ref, o_ref):
    h = pl.program_id(0)
    d = pl.program_id(1)
    t = ATT_T
    i = lax.broadcasted_iota(jnp.int32, (t, t), 0)
    j = lax.broadcasted_iota(jnp.int32, (t, t), 1)
    rel = d * t + i - j
    n = jnp.maximum(rel, 0)
    max_exact = N_BUCKETS // 2
    nf = jnp.maximum(n, 1).astype(F32)
    large = max_exact + (jnp.log(nf / max_exact) / math.log(MAX_DISTANCE / max_exact)
                         * (N_BUCKETS - max_exact)).astype(jnp.int32)
    large = jnp.minimum(large, N_BUCKETS - 1)
    bucket = jnp.where(n < max_exact, n, large)
    val = jnp.zeros((t, t), F32)
    for b in range(N_BUCKETS):
        val = jnp.where(bucket == b, rb_ref[b, h], val)
    val = (val - rb_ref[N_BUCKETS - 1, h]) * LOG2E
    val = jnp.where(rel >= 0, val, NEG_INF)
    o_ref[0:t, :] = val
    o_ref[t:2 * t, :] = val


def _bias_tiles(rel_bias):
    t = ATT_T
    return pl.pallas_call(
        _bias_kernel,
        grid=(N_HEADS, 2),
        in_specs=[pl.BlockSpec(memory_space=pltpu.SMEM)],
        out_specs=pl.BlockSpec((None, None, 2 * t, t), lambda h, d: (h, d, 0, 0)),
        out_shape=jax.ShapeDtypeStruct((N_HEADS, 2, 2 * t, t), F32),
        compiler_params=_params(("arbitrary", "arbitrary")),
        name="bias_tiles",
    )(rel_bias)


def _attn_kernel(q_ref, k_ref, v_ref, bias_ref, lam_ref, sg_ref, o_ref, vx_ref, *, lam_init):
    t = ATT_T
    s_len = k_ref.shape[0]
    vx_ref[:, 0:V_DIM] = v_ref[...]
    vx_ref[:, V_DIM:2 * V_DIM] = jnp.ones((s_len, V_DIM), BF16)
    lam = lam_ref[...]
    lam_full = (jnp.exp(jnp.sum(lam[0:1, :] * lam[1:2, :], axis=-1, keepdims=True))
                - jnp.exp(jnp.sum(lam[2:3, :] * lam[3:4, :], axis=-1, keepdims=True)) + lam_init)
    lane = lax.broadcasted_iota(jnp.int32, (t, V_DIM), 1)
    nt = (((1,), (1,)), ((), ()))

    def logits(qi):
        q = q_ref[qi * t:(qi + 1) * t, :]
        zero = jnp.zeros_like(q)
        q2 = jnp.concatenate([jnp.where(lane < HEAD_DIM, q, zero),
                              jnp.where(lane >= HEAD_DIM, q, zero)], axis=0)
        ranges = []
        if qi >= 2:
            ranges.append((0, (qi - 1) * t, None))
        if qi >= 1:
            ranges.append(((qi - 1) * t, t, 1))
        ranges.append((qi * t, t, 0))
        parts = []
        m = None
        for k0, n, bias_idx in ranges:
            sp = lax.dot_general(q2, k_ref[k0:k0 + n, :], nt, preferred_element_type=F32)
            if bias_idx is not None:
                sp = sp + bias_ref[bias_idx]
            parts.append((k0, sp))
            mp = jnp.max(sp, axis=1, keepdims=True)
            m = mp if m is None else jnp.maximum(m, mp)
        return parts, m

    order = list(reversed(range(s_len // t)))
    pending = logits(order[0])
    for idx, qi in enumerate(order):
        parts, m = pending
        if idx + 1 < len(order):
            pending = logits(order[idx + 1])
        r = None
        for k0, sp in parts:
            p = jnp.exp2((sp - m).astype(BF16))
            rp = _dot(p, vx_ref[k0:k0 + sp.shape[1], :])
            r = rp if r is None else r + rp
        o = r[:, 0:V_DIM] / r[:, V_DIM:2 * V_DIM]
        out = o[0:t, :] - lam_full * o[t:2 * t, :]
        o_ref[qi * t:(qi + 1) * t, :] = (_rms(out, sg_ref[...]) * (1.0 - lam_init)).astype(o_ref.dtype)


def _diff_attention(qkv, bias, lam, subln, lam_init):
    b, s, _ = qkv.shape
    t = ATT_T
    return pl.pallas_call(
        functools.partial(_attn_kernel, lam_init=lam_init),
        grid=(b, N_HEADS),
        in_specs=[
            pl.BlockSpec((None, s, V_DIM), lambda bi, h: (bi, 0, h)),
            pl.BlockSpec((None, s, V_DIM), lambda bi, h: (bi, 0, N_HEADS + h)),
            pl.BlockSpec((None, s, V_DIM), lambda bi, h: (bi, 0, 2 * N_HEADS + h)),
            pl.BlockSpec((None, 2, 2 * t, t), lambda bi, h: (h, 0, 0, 0)),
            pl.BlockSpec((4, HEAD_DIM), lambda bi, h: (0, 0)),
            pl.BlockSpec((1, V_DIM), lambda bi, h: (0, 0)),
        ],
        out_specs=pl.BlockSpec((None, s, V_DIM), lambda bi, h: (bi, 0, h)),
        out_shape=jax.ShapeDtypeStruct((b, s, N_HEADS * V_DIM), BF16),
        scratch_shapes=[pltpu.VMEM((s, 2 * V_DIM), BF16)],
        compiler_params=_params(("parallel", "arbitrary")),
        name="diff_attention",
    )(qkv, qkv, qkv, bias, lam, subln)


CONV_LANES = 128
CONV_ROWS = 64


def _conv_fill(z_ref, halo, sh_ref):
    sh_ref[0, 0:HALO, :] = halo
    sh_ref[0, HALO:HALO + SEQ_TS, :] = z_ref[...]


def _conv_lanes(ls, w_ref, sh_ref, cv_ref):
    ts = SEQ_TS
    ext = HALO + ts
    off = HALO - (CONV_K - 1)
    for r in range(1, SUBLANES):
        sh_ref[r, 0:ext - SUBLANES, ls] = sh_ref[0, r:r + ext - SUBLANES, ls]
    for c in range(ts // CONV_ROWS):
        acc = None
        for k in range(CONV_K):
            a, r = divmod(off + k, SUBLANES)
            row0 = c * CONV_ROWS + SUBLANES * a
            term = w_ref[k:k + 1, ls] * sh_ref[r, row0:row0 + CONV_ROWS, ls]
            acc = term if acc is None else acc + term
        cv_ref[c * CONV_ROWS:(c + 1) * CONV_ROWS, ls] = acc


def _conv_finish(cv_ref, b_ref, lg_ref, lb_ref):
    c = cv_ref[...] + b_ref[...]
    mu = jnp.mean(c, axis=-1, keepdims=True)
    xc = c - mu
    y = xc * lax.rsqrt(jnp.mean(xc * xc, axis=-1, keepdims=True) + LN_EPS) * lg_ref[...] + lb_ref[...]
    return (y * jax.nn.sigmoid(y)).astype(BF16)


def _window_sums(ext_ref, ls, w, lvl_ref):
    ext = ext_ref.shape[0]
    assert w >= 2 and w & (w - 1) == 0 and SUBLANES * (w.bit_length() - 1) <= HALO
    m, level = 1, 0
    while True:
        lo = SUBLANES * (level + 1)
        if level == 0:
            val = ext_ref[lo:ext, ls] + ext_ref[lo - m:ext - m, ls]
        else:
            src = lvl_ref.at[level % 2]
            val = src[lo:ext, :] + src[lo - m:ext - m, :]
        m, level = 2 * m, level + 1
        if m == w:
            return val[HALO - lo:, :]
        lvl_ref[level % 2, lo:ext, :] = val


def _merge_kernel(x_ref, z_ref, zh_ref, a_ref, u_ref, uh_ref, ng_ref, wg_ref,
                  cw_ref, cb_ref, clg_ref, clb_ref, pw_ref, ao_ref,
                  plw_ref, pls_ref, plo_ref, wo_ref, o_ref,
                  sh_ref, cv_ref, ext_ref, lvl_ref, pp_ref):
    ts = SEQ_TS
    i = pl.program_id(1)
    x = x_ref[...]
    h = _rms(x, ng_ref[2:3, :]).astype(BF16)

    _conv_fill(z_ref, jnp.where(i == 0, 0.0, zh_ref[...]), sh_ref)
    for c in range(0, D_MODEL, CONV_LANES):
        _conv_lanes(slice(c, c + CONV_LANES), cw_ref, sh_ref, cv_ref)
    c_act = _conv_finish(cv_ref, cb_ref, clg_ref, clb_ref)

    ext_ref[0:HALO, :] = jnp.where(i == 0, 0.0, uh_ref[...])
    ext_ref[HALO:HALO + ts, :] = u_ref[...]
    pos = i * ts + lax.broadcasted_iota(jnp.int32, (ts, 1), 0)

    def pool_group(g):
        w = POOL_WINDOWS[g]
        ls = slice(g * POOL_GD, (g + 1) * POOL_GD)
        tot = _window_sums(ext_ref, ls, w, lvl_ref)
        cnt = jnp.minimum(pos + 1, w).astype(F32)
        pooled = (tot / cnt - ext_ref[HALO:HALO + ts, ls]).astype(BF16)
        yg = _dot(pooled, plw_ref[g]) * pls_ref[:, ls]
        pp_ref[:, ls] = yg.astype(BF16)

    y_conv = _dot_cols(c_act, pw_ref)
    pool_group(0)
    y_attn = _dot_cols(a_ref[...], ao_ref)
    pool_group(1)
    merged = jax.nn.sigmoid(_dot_cols(h, wg_ref, col0=0)) * y_conv
    pool_group(2)
    merged = merged + jax.nn.sigmoid(_dot_cols(h, wg_ref, col0=D_MODEL)) * y_attn
    pool_group(3)
    gate_pool = jax.nn.sigmoid(_dot_cols(h, wg_ref, col0=2 * D_MODEL))
    y_pool = _dot_cols(pp_ref[...], plo_ref)
    merged = merged + gate_pool * y_pool
    y = _dot_cols(merged.astype(BF16), wo_ref)
    o_ref[...] = x + _rms(y, ng_ref[3:4, :])


def _merge(x, z, attn, u_pool, ng, w_in, conv_dw, conv_b, conv_lg, conv_lb, conv_pw, attn_o,
           pool_w, pool_scale, pool_o, w_out, l):
    b, s, _ = x.shape
    ts = SEQ_TS
    tile = pl.BlockSpec((None, ts, D_MODEL), lambda bi, i: (bi, i, 0))
    halo = pl.BlockSpec((None, HALO, D_MODEL),
                        lambda bi, i: (bi, jnp.maximum(i * (ts // HALO) - 1, 0), 0))
    sq = _resident((D_MODEL, D_MODEL), lambda bi, i: (0, 0))
    vec = _resident((1, D_MODEL), lambda bi, i: (0, 0))
    return pl.pallas_call(
        _merge_kernel,
        grid=(b, s // ts),
        in_specs=[
            tile, tile, halo, tile, tile, halo,
            _resident((None, 8, D_MODEL), lambda bi, i: (l, 0, 0)),
            _resident((D_MODEL, 3 * D_MODEL), lambda bi, i: (0, COL_GATES // (3 * D_MODEL))),
            _resident((CONV_K, D_MODEL), lambda bi, i: (0, 0)),
            vec, vec, vec,
            sq, sq,
            _resident((len(POOL_WINDOWS), POOL_GD, POOL_GD), lambda bi, i: (0, 0, 0)),
            vec,
            sq, sq,
        ],
        out_specs=tile,
        out_shape=jax.ShapeDtypeStruct((b, s, D_MODEL), F32),
        scratch_shapes=[
            pltpu.VMEM((SUBLANES, HALO + ts, D_MODEL), F32),
            pltpu.VMEM((ts, D_MODEL), F32),
            pltpu.VMEM((HALO + ts, D_MODEL), F32),
            pltpu.VMEM((2, HALO + ts, POOL_GD), F32),
            pltpu.VMEM((ts, D_MODEL), BF16),
        ],
        compiler_params=_params(("parallel", "arbitrary")),
        name="merge",
    )(x, z, z, attn, u_pool, u_pool, ng, w_in, conv_dw, conv_b, conv_lg, conv_lb, conv_pw, attn_o,
      pool_w, pool_scale, pool_o, w_out)


def _xkv_kernel(m_ref, g_ref, w_ref, k_ref, v_ref):
    mn = _rms(m_ref[...], g_ref[...]).astype(BF16)
    k_ref[...] = _dot_cols(mn, w_ref, col0=0).astype(k_ref.dtype)
    v_ref[...] = _dot_cols(mn, w_ref, col0=D_MODEL).astype(v_ref.dtype)


def _xattn_kv(mem, g, wkv, l):
    b, m, _ = mem.shape
    out = pl.BlockSpec((None, m, D_MODEL), lambda bi: (bi, 0, 0))
    return pl.pallas_call(
        _xkv_kernel,
        grid=(b,),
        in_specs=[
            pl.BlockSpec((None, m, D_MODEL), lambda bi: (bi, 0, 0)),
            pl.BlockSpec((1, D_MODEL), lambda bi: (0, 0)),
            _resident((D_MODEL, 2 * D_MODEL), lambda bi: (0, 0)),
        ],
        out_specs=[out, out],
        out_shape=[jax.ShapeDtypeStruct((b, m, D_MODEL), BF16)] * 2,
        compiler_params=_params(("parallel",)),
        name="xattn_kv",
    )(mem, g, wkv)


def _xattn_kernel(x_ref, k_ref, v_ref, ng_ref, wq_ref, wo_ref, o_ref, oh_ref):
    x = x_ref[...]
    h = _rms(x, ng_ref[4:5, :]).astype(BF16)
    q = (_dot_cols(h, wq_ref) * (X_HEAD_DIM ** -0.5)).astype(BF16)
    heads = [slice(hd * X_HEAD_DIM, (hd + 1) * X_HEAD_DIM) for hd in range(X_HEADS)]
    logits = [lax.dot_general(q[:, ls], k_ref[:, ls], (((1,), (1,)), ((), ())),
                              preferred_element_type=F32) for ls in heads]
    for ls, s in zip(heads, logits):
        p = jnp.exp(s - jnp.max(s, axis=-1, keepdims=True))
        denom = jnp.sum(p, axis=-1, keepdims=True)
        oh_ref[:, ls] = (_dot(p.astype(BF16), v_ref[:, ls]) / denom).astype(BF16)
    y = _dot_cols(oh_ref[...], wo_ref)
    o_ref[...] = x + _rms(y, ng_ref[5:6, :])


def _xattn(x, k, v, ng, wq, wo, l):
    b, s, _ = x.shape
    m = k.shape[1]
    ts = XATT_TS
    tile = pl.BlockSpec((None, ts, D_MODEL), lambda bi, i: (bi, i, 0))
    kv = pl.BlockSpec((None, m, D_MODEL), lambda bi, i: (bi, 0, 0))
    sq = _resident((D_MODEL, D_MODEL), lambda bi, i: (0, 0))
    return pl.pallas_call(
        _xattn_kernel,
        grid=(b, s // ts),
        in_specs=[tile, kv, kv, _resident((None, 8, D_MODEL), lambda bi, i: (l, 0, 0)), sq, sq],
        out_specs=tile,
        out_shape=jax.ShapeDtypeStruct((b, s, D_MODEL), F32),
        scratch_shapes=[pltpu.VMEM((ts, D_MODEL), BF16)],
        compiler_params=_params(("parallel", "arbitrary")),
        name="xattn",
    )(x, k, v, ng, wq, wo)


def kernel(x, mem, norm_g, ffn_w1, ffn_w3, ffn_w2, w_in, conv_dw, conv_dw_b, conv_ln_g, conv_ln_b, conv_pw, attn_lam, attn_subln, attn_o, rel_bias, pool_w, pool_scale, pool_o, w_out, mem_norm, xattn_q, xattn_kv, xattn_o):
    b, s, d = x.shape
    n = b * s
    ffn_w = (ffn_w1[0, 0].astype(BF16), ffn_w3[0, 0].astype(BF16), ffn_w2[0, 0].astype(BF16))
    layer_f32 = [w_in, conv_pw, attn_o, pool_o, w_out, xattn_q, xattn_o, xattn_kv,
                 pool_w.reshape(DEPTH, d, POOL_GD)]
    bias = _bias_tiles(rel_bias)

    for l in range(DEPTH):
        lam_init = 0.8 - 0.6 * math.exp(-0.3 * l)
        x2, *casts = _ffn(x.reshape(n, d), norm_g, *ffn_w, l, 0,
                          cast_next=(ffn_w1, ffn_w3, ffn_w2, l, 1), cast_layer=(layer_f32, l))
        ffn_w = casts[:3]
        w_in_b, conv_pw_b, attn_o_b, pool_o_b, w_out_b, xq_b, xo_b, xkv_b, pool_w_b = casts[3:]
        pool_w_b = pool_w_b.reshape(len(POOL_WINDOWS), POOL_GD, POOL_GD)
        z, qkv, u_pool = _in_proj(x2, norm_g, w_in_b, l)
        attn = _diff_attention(qkv.reshape(b, s, 3 * d), bias, attn_lam[l], attn_subln[l][None, :],
                               lam_init)
        x = _merge(x2.reshape(b, s, d), z.reshape(b, s, d), attn, u_pool.reshape(b, s, d), norm_g,
                   w_in_b, conv_dw[l], conv_dw_b[l][None, :], conv_ln_g[l][None, :],
                   conv_ln_b[l][None, :], conv_pw_b, attn_o_b, pool_w_b, pool_scale[l][None, :],
                   pool_o_b, w_out_b, l)
        xk, xv = _xattn_kv(mem, mem_norm[l][None, :], xkv_b, l)
        x = _xattn(x, xk, xv, norm_g, xq_b, xo_b, l)
        nxt = (ffn_w1, ffn_w3, ffn_w2, l + 1, 0) if l + 1 < DEPTH else None
        x, *ffn_w = _ffn(x.reshape(n, d), norm_g, *ffn_w, l, 1, cast_next=nxt)
        x = x.reshape(b, s, d)
    return x
```

```python
import functools
import math

import jax
import jax.numpy as jnp
from jax import lax
from jax.experimental import pallas as pl
from jax.experimental.pallas import tpu as pltpu

D_MODEL = 1024
DEPTH = 4
D_FF = 2816
CONV_K = 31
N_HEADS = 8
HEAD_DIM = 64
V_DIM = 2 * HEAD_DIM
N_BUCKETS = 32
MAX_DISTANCE = 128
POOL_WINDOWS = (2, 4, 8, 16)
POOL_GD = D_MODEL // len(POOL_WINDOWS)
X_HEADS = 4
X_HEAD_DIM = D_MODEL // X_HEADS
COL_CONV_A = 0
COL_CONV_G = D_MODEL
COL_QKV = 2 * D_MODEL
COL_POOL = 5 * D_MODEL
COL_GATES = 6 * D_MODEL
RMS_EPS = 1e-6
LN_EPS = 1e-5
NEG_INF = -1e30
LOG2E = math.log2(math.e)

F32 = jnp.float32
BF16 = jnp.bfloat16

MXU_N = 256
SUBLANES = 8
HALO = 32
FFN_TM = 512
FFN_TF = 256
PROJ_TM = 512
SEQ_TS = 256
XATT_TS = 512
ATT_T = 256
VMEM_LIMIT = 56 * 1024 * 1024


def _rms(x, g):
    ms = jnp.mean(x * x, axis=-1, keepdims=True)
    return x * lax.rsqrt(ms + RMS_EPS) * g


def _dot(a, b):
    return jnp.dot(a, b, preferred_element_type=F32)


def _dot_cols(a, w_ref, rows=slice(None), col0=0, ncols=D_MODEL):
    return jnp.concatenate(
        [_dot(a, w_ref[rows, c:c + MXU_N]) for c in range(col0, col0 + ncols, MXU_N)], axis=1)


def _params(sem):
    return pltpu.CompilerParams(dimension_semantics=sem, vmem_limit_bytes=VMEM_LIMIT)


def _resident(shape, index_map):
    return pl.BlockSpec(shape, index_map, pipeline_mode=pl.Buffered(1))


FFN_CAST_ROWS = 128


def _ffn_kernel(x_ref, ng_ref, w1_ref, w3_ref, w2_ref, *rest, g_in, g_out, n_cast):
    src_refs, o_ref, dst_refs = rest[:n_cast], rest[n_cast], rest[n_cast + 1:]
    for src_ref, dst_ref in zip(src_refs, dst_refs):
        dst_ref[...] = src_ref[...].astype(BF16)
    x = x_ref[...]
    h = _rms(x, ng_ref[g_in:g_in + 1, :]).astype(BF16)
    acc = None
    for c in range(D_FF // FFN_TF):
        sl = slice(c * FFN_TF, (c + 1) * FFN_TF)
        a = _dot(h, w1_ref[:, sl])
        b = _dot(h, w3_ref[:, sl])
        t = (a * jax.nn.sigmoid(a) * b).astype(BF16)
        y = _dot_cols(t, w2_ref, rows=sl)
        acc = y if acc is None else acc + y
    o_ref[...] = x + 0.5 * _rms(acc, ng_ref[g_out:g_out + 1, :])


def _ffn(x2, ng, w1, w3, w2, l, s, cast_next=None, cast_layer=None):
    n = x2.shape[0]
    steps = n // FFN_TM
    up_rows = D_MODEL // steps
    assert D_MODEL % steps == 0 and up_rows % 16 == 0
    g_in, g_out = (0, 1) if s == 0 else (6, 7)
    row = pl.BlockSpec((FFN_TM, D_MODEL), lambda i: (i, 0))
    in_specs = [
        row,
        _resident((None, 8, D_MODEL), lambda i: (l, 0, 0)),
        _resident((D_MODEL, D_FF), lambda i: (0, 0)),
        _resident((D_MODEL, D_FF), lambda i: (0, 0)),
        _resident((D_FF, D_MODEL), lambda i: (0, 0)),
    ]
    out_specs = [row]
    out_shape = [jax.ShapeDtypeStruct((n, D_MODEL), F32)]
    args = [x2, ng, w1, w3, w2]
    if cast_next is not None:
        w1f, w3f, w2f, l2, s2 = cast_next
        dn_blocks = D_FF // FFN_CAST_ROWS
        assert D_FF % FFN_CAST_ROWS == 0 and dn_blocks <= steps
        in_specs += [
            pl.BlockSpec((None, None, up_rows, D_FF), lambda i: (l2, s2, i, 0)),
            pl.BlockSpec((None, None, up_rows, D_FF), lambda i: (l2, s2, i, 0)),
            pl.BlockSpec((None, None, FFN_CAST_ROWS, D_MODEL),
                         lambda i: (l2, s2, jnp.minimum(i, dn_blocks - 1), 0)),
        ]
        out_specs += [
            pl.BlockSpec((up_rows, D_FF), lambda i: (i, 0)),
            pl.BlockSpec((up_rows, D_FF), lambda i: (i, 0)),
            pl.BlockSpec((FFN_CAST_ROWS, D_MODEL), lambda i: (jnp.minimum(i, dn_blocks - 1), 0)),
        ]
        out_shape += [jax.ShapeDtypeStruct((D_MODEL, D_FF), BF16),
                      jax.ShapeDtypeStruct((D_MODEL, D_FF), BF16),
                      jax.ShapeDtypeStruct((D_FF, D_MODEL), BF16)]
        args += [w1f, w3f, w2f]
    if cast_layer is not None:
        arrays, l3 = cast_layer
        for arr in arrays:
            cols = arr.shape[-1]
            assert arr.ndim == 3 and arr.shape[1] == D_MODEL
            in_specs.append(pl.BlockSpec((None, up_rows, cols), lambda i: (l3, i, 0)))
            out_specs.append(pl.BlockSpec((up_rows, cols), lambda i: (i, 0)))
            out_shape.append(jax.ShapeDtypeStruct((D_MODEL, cols), BF16))
            args.append(arr)
    return pl.pallas_call(
        functools.partial(_ffn_kernel, g_in=g_in, g_out=g_out, n_cast=len(args) - 5),
        grid=(steps,),
        in_specs=in_specs,
        out_specs=out_specs,
        out_shape=out_shape,
        compiler_params=_params(("arbitrary",)),
        name="ffn",
    )(*args)


def _in_proj_kernel(x_ref, ng_ref, w_ref, z_ref, qkv_ref, u_ref):
    d = D_MODEL
    h = _rms(x_ref[...], ng_ref[2:3, :]).astype(BF16)
    a = _dot_cols(h, w_ref, col0=COL_CONV_A)
    gt = _dot_cols(h, w_ref, col0=COL_CONV_G)
    z_ref[...] = a * jax.nn.sigmoid(gt)
    q = _dot_cols(h, w_ref, col0=COL_QKV) * (HEAD_DIM ** -0.5 * LOG2E)
    qkv_ref[:, 0:d] = q.astype(BF16)
    qkv_ref[:, d:3 * d] = _dot_cols(h, w_ref, col0=COL_QKV + d, ncols=2 * d).astype(BF16)
    u_ref[...] = _dot_cols(h, w_ref, col0=COL_POOL)


def _in_proj(x2, ng, w_in, l):
    n = x2.shape[0]
    row = pl.BlockSpec((PROJ_TM, D_MODEL), lambda i: (i, 0))
    return pl.pallas_call(
        _in_proj_kernel,
        grid=(n // PROJ_TM,),
        in_specs=[
            row,
            _resident((None, 8, D_MODEL), lambda i: (l, 0, 0)),
            _resident((D_MODEL, COL_GATES), lambda i: (0, 0)),
        ],
        out_specs=[row, pl.BlockSpec((PROJ_TM, 3 * D_MODEL), lambda i: (i, 0)), row],
        out_shape=[jax.ShapeDtypeStruct((n, D_MODEL), F32),
                   jax.ShapeDtypeStruct((n, 3 * D_MODEL), BF16),
                   jax.ShapeDtypeStruct((n, D_MODEL), F32)],
        compiler_params=_params(("parallel",)),
        name="in_proj",
    )(x2, ng, w_in)


def _bias_kernel(rb_ref, o_ref):
    h = pl.program_id(0)
    d = pl.program_id(1)
    t = ATT_T
    i = lax.broadcasted_iota(jnp.int32, (t, t), 0)
    j = lax.broadcasted_iota(jnp.int32, (t, t), 1)
    rel = d * t + i - j
    n = jnp.maximum(rel, 0)
    max_exact = N_BUCKETS // 2
    nf = jnp.maximum(n, 1).astype(F32)
    large = max_exact + (jnp.log(nf / max_exact) / math.log(MAX_DISTANCE / max_exact)
                         * (N_BUCKETS - max_exact)).astype(jnp.int32)
    large = jnp.minimum(large, N_BUCKETS - 1)
    bucket = jnp.where(n < max_exact, n, large)
    val = jnp.zeros((t, t), F32)
    for b in range(N_BUCKETS):
        val = jnp.where(bucket == b, rb_ref[b, h], val)
    val = (val - rb_ref[N_BUCKETS - 1, h]) * LOG2E
    val = jnp.where(rel >= 0, val, NEG_INF)
    o_ref[0:t, :] = val
    o_ref[t:2 * t, :] = val


def _bias_tiles(rel_bias):
    t = ATT_T
    return pl.pallas_call(
        _bias_kernel,
        grid=(N_HEADS, 2),
        in_specs=[pl.BlockSpec(memory_space=pltpu.SMEM)],
        out_specs=pl.BlockSpec((None, None, 2 * t, t), lambda h, d: (h, d, 0, 0)),
        out_shape=jax.ShapeDtypeStruct((N_HEADS, 2, 2 * t, t), F32),
        compiler_params=_params(("arbitrary", "arbitrary")),
        name="bias_tiles",
    )(rel_bias)


def _attn_kernel(q_ref, k_ref, v_ref, bias_ref, lam_ref, sg_ref, o_ref, vx_ref, *, lam_init):
    t = ATT_T
    s_len = k_ref.shape[0]
    vx_ref[:, 0:V_DIM] = v_ref[...]
    vx_ref[:, V_DIM:2 * V_DIM] = jnp.ones((s_len, V_DIM), BF16)
    lam = lam_ref[...]
    lam_full = (jnp.exp(jnp.sum(lam[0:1, :] * lam[1:2, :], axis=-1, keepdims=True))
                - jnp.exp(jnp.sum(lam[2:3, :] * lam[3:4, :], axis=-1, keepdims=True)) + lam_init)
    lane = lax.broadcasted_iota(jnp.int32, (t, V_DIM), 1)
    nt = (((1,), (1,)), ((), ()))

    def logits(qi):
        q = q_ref[qi * t:(qi + 1) * t, :]
        zero = jnp.zeros_like(q)
        q2 = jnp.concatenate([jnp.where(lane < HEAD_DIM, q, zero),
                              jnp.where(lane >= HEAD_DIM, q, zero)], axis=0)
        ranges = []
        if qi >= 2:
            ranges.append((0, (qi - 1) * t, None))
        if qi >= 1:
            ranges.append(((qi - 1) * t, t, 1))
        ranges.append((qi * t, t, 0))
        parts = []
        m = None
        for k0, n, bias_idx in ranges:
            sp = lax.dot_general(q2, k_ref[k0:k0 + n, :], nt, preferred_element_type=F32)
            if bias_idx is not None:
                sp = sp + bias_ref[bias_idx]
            parts.append((k0, sp))
            mp = jnp.max(sp, axis=1, keepdims=True)
            m = mp if m is None else jnp.maximum(m, mp)
        return parts, m

    order = list(reversed(range(s_len // t)))
    pending = logits(order[0])
    for idx, qi in enumerate(order):
        parts, m = pending
        if idx + 1 < len(order):
            pending = logits(order[idx + 1])
        r = None
        for k0, sp in parts:
            p = jnp.exp2((sp - m).astype(BF16))
            rp = _dot(p, vx_ref[k0:k0 + sp.shape[1], :])
            r = rp if r is None else r + rp
        o = r[:, 0:V_DIM] / r[:, V_DIM:2 * V_DIM]
        out = o[0:t, :] - lam_full * o[t:2 * t, :]
        o_ref[qi * t:(qi + 1) * t, :] = (_rms(out, sg_ref[...]) * (1.0 - lam_init)).astype(o_ref.dtype)


def _diff_attention(qkv, bias, lam, subln, lam_init):
    b, s, _ = qkv.shape
    t = ATT_T
    return pl.pallas_call(
        functools.partial(_attn_kernel, lam_init=lam_init),
        grid=(b, N_HEADS),
        in_specs=[
            pl.BlockSpec((None, s, V_DIM), lambda bi, h: (bi, 0, h)),
            pl.BlockSpec((None, s, V_DIM), lambda bi, h: (bi, 0, N_HEADS + h)),
            pl.BlockSpec((None, s, V_DIM), lambda bi, h: (bi, 0, 2 * N_HEADS + h)),
            pl.BlockSpec((None, 2, 2 * t, t), lambda bi, h: (h, 0, 0, 0)),
            pl.BlockSpec((4, HEAD_DIM), lambda bi, h: (0, 0)),
            pl.BlockSpec((1, V_DIM), lambda bi, h: (0, 0)),
        ],
        out_specs=pl.BlockSpec((None, s, V_DIM), lambda bi, h: (bi, 0, h)),
        out_shape=jax.ShapeDtypeStruct((b, s, N_HEADS * V_DIM), BF16),
        scratch_shapes=[pltpu.VMEM((s, 2 * V_DIM), BF16)],
        compiler_params=_params(("parallel", "arbitrary")),
        name="diff_attention",
    )(qkv, qkv, qkv, bias, lam, subln)


CONV_LANES = 128
CONV_ROWS = 64


def _conv_fill(z_ref, halo, sh_ref):
    sh_ref[0, 0:HALO, :] = halo
    sh_ref[0, HALO:HALO + SEQ_TS, :] = z_ref[...]


def _conv_lanes(ls, w_ref, sh_ref, cv_ref):
    ts = SEQ_TS
    ext = HALO + ts
    off = HALO - (CONV_K - 1)
    for r in range(1, SUBLANES):
        sh_ref[r, 0:ext - SUBLANES, ls] = sh_ref[0, r:r + ext - SUBLANES, ls]
    for c in range(ts // CONV_ROWS):
        acc = None
        for k in range(CONV_K):
            a, r = divmod(off + k, SUBLANES)
            row0 = c * CONV_ROWS + SUBLANES * a
            term = w_ref[k:k + 1, ls] * sh_ref[r, row0:row0 + CONV_ROWS, ls]
            acc = term if acc is None else acc + term
        cv_ref[c * CONV_ROWS:(c + 1) * CONV_ROWS, ls] = acc


def _conv_finish(cv_ref, b_ref, lg_ref, lb_ref):
    c = cv_ref[...] + b_ref[...]
    mu = jnp.mean(c, axis=-1, keepdims=True)
    xc = c - mu
    y = xc * lax.rsqrt(jnp.mean(xc * xc, axis=-1, keepdims=True) + LN_EPS) * lg_ref[...] + lb_ref[...]
    return (y * jax.nn.sigmoid(y)).astype(BF16)


def _window_sums(ext_ref, ls, w, lvl_ref):
    ext = ext_ref.shape[0]
    assert w >= 2 and w & (w - 1) == 0 and SUBLANES * (w.bit_length() - 1) <= HALO
    m, level = 1, 0
    while True:
        lo = SUBLANES * (level + 1)
        if level == 0:
            val = ext_ref[lo:ext, ls] + ext_ref[lo - m:ext - m, ls]
        else:
            src = lvl_ref.at[level % 2]
            val = src[lo:ext, :] + src[lo - m:ext - m, :]
        m, level = 2 * m, level + 1
        if m == w:
            return val[HALO - lo:, :]
        lvl_ref[level % 2, lo:ext, :] = val


def _merge_kernel(x_ref, z_ref, zh_ref, a_ref, u_ref, uh_ref, ng_ref, wg_ref,
                  cw_ref, cb_ref, clg_ref, clb_ref, pw_ref, ao_ref,
                  plw_ref, pls_ref, plo_ref, wo_ref, o_ref,
                  sh_ref, cv_ref, ext_ref, lvl_ref, pp_ref, mg_ref, mb_ref):
    ts = SEQ_TS
    i = pl.program_id(1)
    x = x_ref[...]
    h = _rms(x, ng_ref[2:3, :]).astype(BF16)

    _conv_fill(z_ref, jnp.where(i == 0, 0.0, zh_ref[...]), sh_ref)
    for c in range(0, D_MODEL, CONV_LANES):
        _conv_lanes(slice(c, c + CONV_LANES), cw_ref, sh_ref, cv_ref)
    c_act = _conv_finish(cv_ref, cb_ref, clg_ref, clb_ref)

    ext_ref[0:HALO, :] = jnp.where(i == 0, 0.0, uh_ref[...])
    ext_ref[HALO:HALO + ts, :] = u_ref[...]
    pos = i * ts + lax.broadcasted_iota(jnp.int32, (ts, 1), 0)

    def pool_group(g):
        w = POOL_WINDOWS[g]
        ls = slice(g * POOL_GD, (g + 1) * POOL_GD)
        tot = _window_sums(ext_ref, ls, w, lvl_ref)
        cnt = jnp.minimum(pos + 1, w).astype(F32)
        pooled = (tot / cnt - ext_ref[HALO:HALO + ts, ls]).astype(BF16)
        yg = _dot(pooled, plw_ref[g]) * pls_ref[:, ls]
        pp_ref[:, ls] = yg.astype(BF16)

    a_in = a_ref[...]
    for g, c in enumerate(range(0, D_MODEL, MXU_N)):
        cs = slice(c, c + MXU_N)
        part = jax.nn.sigmoid(_dot(h, wg_ref[:, cs])) * _dot(c_act, pw_ref[:, cs])
        part = part + (jax.nn.sigmoid(_dot(h, wg_ref[:, D_MODEL + c:D_MODEL + c + MXU_N]))
                       * _dot(a_in, ao_ref[:, cs]))
        mg_ref[:, cs] = part
        pool_group(g)
    pooled_in = pp_ref[...]
    for c in range(0, D_MODEL, MXU_N):
        cs = slice(c, c + MXU_N)
        gate = jax.nn.sigmoid(_dot(h, wg_ref[:, 2 * D_MODEL + c:2 * D_MODEL + c + MXU_N]))
        mb_ref[:, cs] = (mg_ref[:, cs] + gate * _dot(pooled_in, plo_ref[:, cs])).astype(BF16)
    y = _dot_cols(mb_ref[...], wo_ref)
    o_ref[...] = x + _rms(y, ng_ref[3:4, :])


def _merge(x, z, attn, u_pool, ng, w_in, conv_dw, conv_b, conv_lg, conv_lb, conv_pw, attn_o,
           pool_w, pool_scale, pool_o, w_out, l):
    b, s, _ = x.shape
    ts = SEQ_TS
    tile = pl.BlockSpec((None, ts, D_MODEL), lambda bi, i: (bi, i, 0))
    halo = pl.BlockSpec((None, HALO, D_MODEL),
                        lambda bi, i: (bi, jnp.maximum(i * (ts // HALO) - 1, 0), 0))
    sq = _resident((D_MODEL, D_MODEL), lambda bi, i: (0, 0))
    vec = _resident((1, D_MODEL), lambda bi, i: (0, 0))
    return pl.pallas_call(
        _merge_kernel,
        grid=(b, s // ts),
        in_specs=[
            tile, tile, halo, tile, tile, halo,
            _resident((None, 8, D_MODEL), lambda bi, i: (l, 0, 0)),
            _resident((D_MODEL, 3 * D_MODEL), lambda bi, i: (0, COL_GATES // (3 * D_MODEL))),
            _resident((CONV_K, D_MODEL), lambda bi, i: (0, 0)),
            vec, vec, vec,
            sq, sq,
            _resident((len(POOL_WINDOWS), POOL_GD, POOL_GD), lambda bi, i: (0, 0, 0)),
            vec,
            sq, sq,
        ],
        out_specs=tile,
        out_shape=jax.ShapeDtypeStruct((b, s, D_MODEL), F32),
        scratch_shapes=[
            pltpu.VMEM((SUBLANES, HALO + ts, D_MODEL), F32),
            pltpu.VMEM((ts, D_MODEL), F32),
            pltpu.VMEM((HALO + ts, D_MODEL), F32),
            pltpu.VMEM((2, HALO + ts, POOL_GD), F32),
            pltpu.VMEM((ts, D_MODEL), BF16),
            pltpu.VMEM((ts, D_MODEL), F32),
            pltpu.VMEM((ts, D_MODEL), BF16),
        ],
        compiler_params=_params(("parallel", "arbitrary")),
        name="merge",
    )(x, z, z, attn, u_pool, u_pool, ng, w_in, conv_dw, conv_b, conv_lg, conv_lb, conv_pw, attn_o,
      pool_w, pool_scale, pool_o, w_out)


def _xkv_kernel(m_ref, g_ref, w_ref, k_ref, v_ref):
    mn = _rms(m_ref[...], g_ref[...]).astype(BF16)
    k_ref[...] = _dot_cols(mn, w_ref, col0=0).astype(k_ref.dtype)
    v_ref[...] = _dot_cols(mn, w_ref, col0=D_MODEL).astype(v_ref.dtype)


def _xattn_kv(mem, g, wkv, l):
    b, m, _ = mem.shape
    out = pl.BlockSpec((None, m, D_MODEL), lambda bi: (bi, 0, 0))
    return pl.pallas_call(
        _xkv_kernel,
        grid=(b,),
        in_specs=[
            pl.BlockSpec((None, m, D_MODEL), lambda bi: (bi, 0, 0)),
            pl.BlockSpec((1, D_MODEL), lambda bi: (0, 0)),
            _resident((D_MODEL, 2 * D_MODEL), lambda bi: (0, 0)),
        ],
        out_specs=[out, out],
        out_shape=[jax.ShapeDtypeStruct((b, m, D_MODEL), BF16)] * 2,
        compiler_params=_params(("parallel",)),
        name="xattn_kv",
    )(mem, g, wkv)


def _xattn_kernel(x_ref, k_ref, v_ref, ng_ref, wq_ref, wo_ref, o_ref, oh_ref):
    x = x_ref[...]
    h = _rms(x, ng_ref[4:5, :]).astype(BF16)
    q = (_dot_cols(h, wq_ref) * (X_HEAD_DIM ** -0.5)).astype(BF16)
    heads = [slice(hd * X_HEAD_DIM, (hd + 1) * X_HEAD_DIM) for hd in range(X_HEADS)]
    logits = [lax.dot_general(q[:, ls], k_ref[:, ls], (((1,), (1,)), ((), ())),
                              preferred_element_type=F32) for ls in heads]
    for ls, s in zip(heads, logits):
        p = jnp.exp(s - jnp.max(s, axis=-1, keepdims=True))
        denom = jnp.sum(p, axis=-1, keepdims=True)
        oh_ref[:, ls] = (_dot(p.astype(BF16), v_ref[:, ls]) / denom).astype(BF16)
    y = _dot_cols(oh_ref[...], wo_ref)
    o_ref[...] = x + _rms(y, ng_ref[5:6, :])


def _xattn(x, k, v, ng, wq, wo, l):
    b, s, _ = x.shape
    m = k.shape[1]
    ts = XATT_TS
    tile = pl.BlockSpec((None, ts, D_MODEL), lambda bi, i: (bi, i, 0))
    kv = pl.BlockSpec((None, m, D_MODEL), lambda bi, i: (bi, 0, 0))
    sq = _resident((D_MODEL, D_MODEL), lambda bi, i: (0, 0))
    return pl.pallas_call(
        _xattn_kernel,
        grid=(b, s // ts),
        in_specs=[tile, kv, kv, _resident((None, 8, D_MODEL), lambda bi, i: (l, 0, 0)), sq, sq],
        out_specs=tile,
        out_shape=jax.ShapeDtypeStruct((b, s, D_MODEL), F32),
        scratch_shapes=[pltpu.VMEM((ts, D_MODEL), BF16)],
        compiler_params=_params(("parallel", "arbitrary")),
        name="xattn",
    )(x, k, v, ng, wq, wo)


def kernel(x, mem, norm_g, ffn_w1, ffn_w3, ffn_w2, w_in, conv_dw, conv_dw_b, conv_ln_g, conv_ln_b, conv_pw, attn_lam, attn_subln, attn_o, rel_bias, pool_w, pool_scale, pool_o, w_out, mem_norm, xattn_q, xattn_kv, xattn_o):
    b, s, d = x.shape
    n = b * s
    ffn_w = (ffn_w1[0, 0].astype(BF16), ffn_w3[0, 0].astype(BF16), ffn_w2[0, 0].astype(BF16))
    layer_f32 = [w_in, conv_pw, attn_o, pool_o, w_out, xattn_q, xattn_o, xattn_kv,
                 pool_w.reshape(DEPTH, d, POOL_GD)]
    bias = _bias_tiles(rel_bias)

    for l in range(DEPTH):
        lam_init = 0.8 - 0.6 * math.exp(-0.3 * l)
        x2, *casts = _ffn(x.reshape(n, d), norm_g, *ffn_w, l, 0,
                          cast_next=(ffn_w1, ffn_w3, ffn_w2, l, 1), cast_layer=(layer_f32, l))
        ffn_w = casts[:3]
        w_in_b, conv_pw_b, attn_o_b, pool_o_b, w_out_b, xq_b, xo_b, xkv_b, pool_w_b = casts[3:]
        pool_w_b = pool_w_b.reshape(len(POOL_WINDOWS), POOL_GD, POOL_GD)
        z, qkv, u_pool = _in_proj(x2, norm_g, w_in_b, l)
        attn = _diff_attention(qkv.reshape(b, s, 3 * d), bias, attn_lam[l], attn_subln[l][None, :],
                               lam_init)
        x = _merge(x2.reshape(b, s, d), z.reshape(b, s, d), attn, u_pool.reshape(b, s, d), norm_g,
                   w_in_b, conv_dw[l], conv_dw_b[l][None, :], conv_ln_g[l][None, :],
                   conv_ln_b[l][None, :], conv_pw_b, attn_o_b, pool_w_b, pool_scale[l][None, :],
                   pool_o_b, w_out_b, l)
        xk, xv = _xattn_kv(mem, mem_norm[l][None, :], xkv_b, l)
        x = _xattn(x, xk, xv, norm_g, xq_b, xo_b, l)
        nxt = (ffn_w1, ffn_w3, ffn_w2, l + 1, 0) if l + 1 < DEPTH else None
        x, *ffn_w = _ffn(x.reshape(n, d), norm_g, *ffn_w, l, 1, cast_next=nxt)
        x = x.reshape(b, s, d)
    return x
```
